```python
import jax, jax.numpy as jnp
from jax import lax
import numpy as np

D_MODEL = 2048
BATCH = 8
SEQ = 2048
DEPTH = 2

GRID_W = 64
CTX_LEN = 256
EPS = 1e-6

HEAD_DIM = 128
N_Q_HEADS = D_MODEL // (2 * HEAD_DIM)
N_KV_HEADS = max(1, N_Q_HEADS // 4)
Q_PER_KV = N_Q_HEADS // N_KV_HEADS
ATTN_WIDTH = N_Q_HEADS * HEAD_DIM
KV_WIDTH = N_KV_HEADS * HEAD_DIM
Q_BLOCK = 128
ROPE_THETA = 10000.0
AXIS_DIM = HEAD_DIM // 2

CHUNK = 128
MLP_WIDTH = D_MODEL // 4
MLP_GROUP_W = 128
MLP_GROUPS = MLP_WIDTH // MLP_GROUP_W

LRU_WIDTH = D_MODEL // 4
LRU_BLOCKS = 4
LRU_BLOCK_W = LRU_WIDTH // LRU_BLOCKS
LRU_C = 8.0
CONV_W = 4
CONV_LEFT = 2

MIX_WIDTH = ATTN_WIDTH + MLP_WIDTH + LRU_WIDTH
IN_COLS = ATTN_WIDTH + 2 * KV_WIDTH + 2 * MLP_WIDTH + 2 * LRU_WIDTH
SPLITS = (ATTN_WIDTH,
          ATTN_WIDTH + KV_WIDTH,
          ATTN_WIDTH + 2 * KV_WIDTH,
          ATTN_WIDTH + 2 * KV_WIDTH + 2 * MLP_WIDTH,
          ATTN_WIDTH + 2 * KV_WIDTH + 2 * MLP_WIDTH + LRU_WIDTH)

D_FF = -(-8 * D_MODEL // (3 * 256)) * 256

kernel_name = "hybrid_parallel_heads_dit_block"


def rmsnorm(x, g):
    xf = x.astype(jnp.float32)
    y = xf * lax.rsqrt(jnp.mean(xf * xf, axis=-1, keepdims=True) + EPS)
    return (y * g.astype(jnp.float32)).astype(x.dtype)


def modulate(h, shift, scale):
    return h * (1 + scale) + shift


def axial_rope_tables(n_tokens):
    rows = n_tokens // GRID_W
    row_ids = jnp.repeat(jnp.arange(rows, dtype=jnp.float32), GRID_W)
    col_ids = jnp.tile(jnp.arange(GRID_W, dtype=jnp.float32), rows)
    inv_freq = ROPE_THETA ** (-jnp.arange(0, AXIS_DIM, 2, dtype=jnp.float32) / AXIS_DIM)
    ang_r = row_ids[:, None] * inv_freq
    ang_c = col_ids[:, None] * inv_freq
    ang = jnp.concatenate([ang_r, ang_r, ang_c, ang_c], axis=-1)
    return jnp.cos(ang), jnp.sin(ang)


def apply_axial_rope(x, cos, sin):
    shape = x.shape
    xf = x.astype(jnp.float32)
    xs = xf.reshape(shape[:-1] + (2, 2, AXIS_DIM // 2))
    rot = jnp.stack([-xs[..., 1, :], xs[..., 0, :]], axis=-2).reshape(shape)
    bshape = (shape[1],) + (1,) * (x.ndim - 3) + (HEAD_DIM,)
    return (xf * cos.reshape(bshape) + rot * sin.reshape(bshape)).astype(x.dtype)


def attend(q, k, v):
    s = jnp.einsum('bkgqd,bktd->bkgqt', q, k, preferred_element_type=jnp.float32) * (HEAD_DIM ** -0.5)
    p = jax.nn.softmax(s, axis=-1).astype(v.dtype)
    return jnp.einsum('bkgqt,bktd->bkgqd', p, v)


def chunk_mlp(z, w_s, b_s):
    B, L, _ = z.shape
    u, v = jnp.split(jax.nn.gelu(z), 2, axis=-1)
    vb = v.reshape(B, L // CHUNK, CHUNK, MLP_GROUPS, MLP_GROUP_W)
    s = jnp.einsum('gpq,bnqgc->bnpgc', w_s, vb) + b_s.T[None, None, :, :, None]
    return u * s.reshape(B, L, MLP_WIDTH)


def depthwise_conv(x, w, b):
    L = x.shape[1]
    xp = jnp.pad(x, ((0, 0), (CONV_LEFT, CONV_W - 1 - CONV_LEFT), (0, 0)))
    y = b
    for tap in range(CONV_W):
        y = y + xp[:, tap:tap + L] * w[tap]
    return y


def lru_coeffs(x, w_a, w_x, b_a, b_x, lam):
    B, L, W = x.shape
    xb = x.reshape(B, L, LRU_BLOCKS, LRU_BLOCK_W)
    r = jax.nn.sigmoid((jnp.einsum('blhi,hij->blhj', xb, w_a).reshape(B, L, W) + b_a).astype(jnp.float32))
    i = jax.nn.sigmoid((jnp.einsum('blhi,hij->blhj', xb, w_x).reshape(B, L, W) + b_x).astype(jnp.float32))
    log_a = -LRU_C * jax.nn.softplus(-lam.astype(jnp.float32)) * r
    a = jnp.exp(log_a)
    bx = jnp.sqrt(-jnp.expm1(2.0 * log_a)) * (i * x.astype(jnp.float32))
    return a, bx


def linear_scan(a, bx, h0, reverse):
    def combine(l, r):
        return (l[0] * r[0], r[0] * l[1] + r[1])
    A, Bc = lax.associative_scan(combine, (a, bx), reverse=reverse, axis=1)
    if h0 is None:
        return Bc
    return A * h0[:, None, :] + Bc


def swiglu(h, w_ffn_in, w_ffn_out):
    g, u = jnp.split(h @ w_ffn_in, 2, axis=-1)
    return (jax.nn.silu(g) * u) @ w_ffn_out


def mixer(h, hc, w_in, g_qk, w_s, b_s, conv_w, conv_b, lru_w, lru_b, lru_lam, w_out, cos, sin, ctx_out):
    B, N, _ = h.shape
    C = hc.shape[1]
    q, k, v, zm, zx, zg = jnp.split(h @ w_in, SPLITS, axis=-1)
    qc, kc, vc, zmc, zxc, zgc = jnp.split(hc @ w_in, SPLITS, axis=-1)

    q = apply_axial_rope(rmsnorm(q.reshape(B, N, N_KV_HEADS, Q_PER_KV, HEAD_DIM), g_qk[0]), cos, sin)
    q = q.transpose(0, 2, 3, 1, 4)
    k = apply_axial_rope(rmsnorm(k.reshape(B, N, N_KV_HEADS, HEAD_DIM), g_qk[1]), cos, sin)
    k = k.transpose(0, 2, 1, 3)
    v = v.reshape(B, N, N_KV_HEADS, HEAD_DIM).transpose(0, 2, 1, 3)
    kc = rmsnorm(kc.reshape(B, C, N_KV_HEADS, HEAD_DIM), g_qk[1]).transpose(0, 2, 1, 3)
    vc = vc.reshape(B, C, N_KV_HEADS, HEAD_DIM).transpose(0, 2, 1, 3)
    k_all = jnp.concatenate([kc, k], axis=2)
    v_all = jnp.concatenate([vc, v], axis=2)
    n_blocks = N // Q_BLOCK
    qb = q.reshape(B, N_KV_HEADS, Q_PER_KV, n_blocks, Q_BLOCK, HEAD_DIM).transpose(3, 0, 1, 2, 4, 5)
    o = lax.map(lambda qi: attend(qi, k_all, v_all), qb)
    attn = o.transpose(1, 0, 4, 2, 3, 5).reshape(B, N, ATTN_WIDTH)

    mlp = chunk_mlp(zm, w_s, b_s)

    xr = depthwise_conv(zx, conv_w, conv_b)
    xrc = depthwise_conv(zxc, conv_w, conv_b)
    h_lat, h_ctx = [], []
    for d, rev in ((0, False), (1, True)):
        ac, bcx = lru_coeffs(xrc, lru_w[d, 0], lru_w[d, 1], lru_b[d, 0], lru_b[d, 1], lru_lam[d])
        hcs = linear_scan(ac, bcx, None, rev)
        h0 = hcs[:, 0] if rev else hcs[:, -1]
        al, blx = lru_coeffs(xr, lru_w[d, 0], lru_w[d, 1], lru_b[d, 0], lru_b[d, 1], lru_lam[d])
        h_lat.append(linear_scan(al, blx, h0, rev))
        h_ctx.append(hcs)
    lru = ((h_lat[0] + h_lat[1]) * jax.nn.gelu(zg.astype(jnp.float32))).astype(h.dtype)

    y = jnp.concatenate([attn, mlp, lru], axis=-1) @ w_out
    if not ctx_out:
        return y, None

    qc = rmsnorm(qc.reshape(B, C, N_KV_HEADS, Q_PER_KV, HEAD_DIM), g_qk[0]).transpose(0, 2, 3, 1, 4)
    attn_c = attend(qc, kc, vc).transpose(0, 3, 1, 2, 4).reshape(B, C, ATTN_WIDTH)
    mlp_c = chunk_mlp(zmc, w_s, b_s)
    lru_c = ((h_ctx[0] + h_ctx[1]) * jax.nn.gelu(zgc.astype(jnp.float32))).astype(hc.dtype)
    yc = jnp.concatenate([attn_c, mlp_c, lru_c], axis=-1) @ w_out
    return y, yc


def setup_inputs(seed: int = 0) -> dict:
    key = jax.random.key(seed)
    ks = jax.random.split(key, 24)
    f32 = jnp.float32

    def nrm(k, shape, scale):
        return jax.random.normal(k, shape, f32) * scale

    u = jax.random.uniform(ks[17], (DEPTH, 2, LRU_WIDTH), f32, minval=0.9, maxval=0.999)
    a0 = u ** (1.0 / LRU_C)
    lru_lam = jnp.log(a0) - jnp.log1p(-a0)
    return {
        "x": nrm(ks[0], (BATCH, SEQ, D_MODEL), 1.0),
        "c": nrm(ks[1], (BATCH, D_MODEL), 1.0),
        "ctx": nrm(ks[2], (BATCH, CTX_LEN, D_MODEL), 1.0),
        "c_ctx": nrm(ks[3], (D_MODEL,), 1.0),
        "w_mod": nrm(ks[4], (DEPTH, D_MODEL, 6 * D_MODEL), 0.5 * D_MODEL ** -0.5),
        "b_mod": nrm(ks[5], (DEPTH, 6 * D_MODEL), 0.02),
        "g_norm": 1.0 + nrm(ks[6], (DEPTH, 4, D_MODEL), 0.02),
        "w_in": nrm(ks[7], (DEPTH, D_MODEL, IN_COLS), D_MODEL ** -0.5),
        "g_qk": 1.0 + nrm(ks[8], (DEPTH, 2, HEAD_DIM), 0.02),
        "w_s": nrm(ks[9], (DEPTH, MLP_GROUPS, CHUNK, CHUNK), CHUNK ** -0.5),
        "b_s": nrm(ks[10], (DEPTH, MLP_GROUPS, CHUNK), 0.1),
        "conv_w": nrm(ks[11], (DEPTH, CONV_W, LRU_WIDTH), CONV_W ** -0.5),
        "conv_b": nrm(ks[12], (DEPTH, LRU_WIDTH), 0.02),
        "lru_w": nrm(ks[13], (DEPTH, 2, 2, LRU_BLOCKS, LRU_BLOCK_W, LRU_BLOCK_W), LRU_BLOCK_W ** -0.5),
        "lru_b": nrm(ks[14], (DEPTH, 2, 2, LRU_WIDTH), 0.1),
        "lru_lam": lru_lam,
        "w_out": nrm(ks[15], (DEPTH, MIX_WIDTH, D_MODEL), MIX_WIDTH ** -0.5),
        "w_ffn_in": nrm(ks[16], (DEPTH, D_MODEL, 2 * D_FF), D_MODEL ** -0.5),
        "w_ffn_out": nrm(ks[18], (DEPTH, D_FF, D_MODEL), D_FF ** -0.5),
    }


def reference(x, c, ctx, c_ctx, w_mod, b_mod, g_norm, w_in, g_qk, w_s, b_s, conv_w, conv_b,
              lru_w, lru_b, lru_lam, w_out, w_ffn_in, w_ffn_out):
    cos, sin = axial_rope_tables(x.shape[1])
    c_act = jax.nn.silu(c)
    cc_act = jax.nn.silu(c_ctx)
    xc = ctx
    for l in range(DEPTH):
        last = l == DEPTH - 1
        mod = (c_act @ w_mod[l] + b_mod[l])[:, None, :]
        modc = cc_act @ w_mod[l] + b_mod[l]
        sh1, sc1, gt1, sh2, sc2, gt2 = jnp.split(mod, 6, axis=-1)
        shc1, scc1, gtc1, shc2, scc2, gtc2 = jnp.split(modc, 6, axis=-1)

        h = modulate(rmsnorm(x, g_norm[l, 0]), sh1, sc1)
        hc = modulate(rmsnorm(xc, g_norm[l, 0]), shc1, scc1)
        y, yc = mixer(h, hc, w_in[l], g_qk[l], w_s[l], b_s[l], conv_w[l], conv_b[l],
                      lru_w[l], lru_b[l], lru_lam[l], w_out[l], cos, sin, not last)
        x = x + gt1 * rmsnorm(y, g_norm[l, 1])
        f = swiglu(modulate(rmsnorm(x, g_norm[l, 2]), sh2, sc2), w_ffn_in[l], w_ffn_out[l])
        x = x + gt2 * rmsnorm(f, g_norm[l, 3])

        if not last:
            xc = xc + gtc1 * rmsnorm(yc, g_norm[l, 1])
            fc = swiglu(modulate(rmsnorm(xc, g_norm[l, 2]), shc2, scc2), w_ffn_in[l], w_ffn_out[l])
            xc = xc + gtc2 * rmsnorm(fc, g_norm[l, 3])
    return x
```

```python
import functools

import jax
import jax.numpy as jnp
from jax import lax
from jax.experimental import pallas as pl
from jax.experimental.pallas import tpu as pltpu

F32 = jnp.float32
BF16 = jnp.bfloat16

D_MODEL = 2048
GRID_W = 64
EPS = 1e-6
HEAD_DIM = 128
N_Q_HEADS = 8
N_KV_HEADS = 2
Q_PER_KV = 4
ATTN_WIDTH = 1024
KV_WIDTH = 256
ROPE_THETA = 10000.0
AXIS_DIM = 64
CHUNK = 128
MLP_WIDTH = 512
MLP_GROUPS = 4
LRU_WIDTH = 512
LRU_BLOCKS = 4
LRU_C = 8.0
IN_COLS = 3584
D_FF = 5632

K_COL128 = ATTN_WIDTH // 128
V_COL128 = (ATTN_WIDTH + KV_WIDTH) // 128
ZM_COL512 = (ATTN_WIDTH + 2 * KV_WIDTH) // 512
ZX_COL512 = ZM_COL512 + 2
ZG_COL512 = ZX_COL512 + 1

VMEM_LIMIT = 56 * 1024 * 1024
MOD_ROWS = 16
LRU_T = 256
LRU_PAD = 8


def _params(sem):
    return pltpu.CompilerParams(dimension_semantics=sem, vmem_limit_bytes=VMEM_LIMIT)


def _rms(xf, g):
    return xf * lax.rsqrt(jnp.mean(xf * xf, axis=-1, keepdims=True) + EPS) * g


def _mod_kernel(c_ref, w_ref, b_ref, o_ref):
    c = c_ref[...]
    a = (c * jax.nn.sigmoid(c)).astype(BF16)
    o_ref[0] = jnp.dot(a, w_ref[0].astype(BF16), preferred_element_type=F32) + b_ref[0]


def _modulation(c_all, w_mod, b_mod):
    depth, d, n = w_mod.shape
    tn = 1024
    return pl.pallas_call(
        _mod_kernel,
        grid=(depth, n // tn),
        in_specs=[
            pl.BlockSpec((MOD_ROWS, d), lambda l, j: (0, 0)),
            pl.BlockSpec((1, d, tn), lambda l, j: (l, 0, j)),
            pl.BlockSpec((1, 1, tn), lambda l, j: (l, 0, j)),
        ],
        out_specs=pl.BlockSpec((1, MOD_ROWS, tn), lambda l, j: (l, 0, j)),
        out_shape=jax.ShapeDtypeStruct((depth, MOD_ROWS, n), F32),
        compiler_params=_params(("arbitrary", "arbitrary")),
        name="modulation",
    )(c_all, w_mod, b_mod.reshape(depth, 1, n))


def _in_proj_kernel(x_ref, mod_ref, g_ref, w_ref, z_ref):
    m = mod_ref[0]
    h = _rms(x_ref[...], g_ref[...]) * (1.0 + m[1:2]) + m[0:1]
    z_ref[...] = jnp.dot(h.astype(BF16), w_ref[...], preferred_element_type=F32).astype(BF16)


def _in_proj(x, mod, g, w, rows_per_mod, tm):
    M = x.shape[0]
    if rows_per_mod is None:
        mod_map = lambda i: (8, 0, 0)
    else:
        blocks = rows_per_mod // tm
        mod_map = lambda i: (i // blocks, 0, 0)
    return pl.pallas_call(
        _in_proj_kernel,
        grid=(M // tm,),
        in_specs=[
            pl.BlockSpec((tm, D_MODEL), lambda i: (i, 0)),
            pl.BlockSpec((1, 6, D_MODEL), mod_map),
            pl.BlockSpec((1, D_MODEL), lambda i: (0, 0)),
            pl.BlockSpec((D_MODEL, IN_COLS), lambda i: (0, 0)),
        ],
        out_specs=pl.BlockSpec((tm, IN_COLS), lambda i: (i, 0)),
        out_shape=jax.ShapeDtypeStruct((M, IN_COLS), BF16),
        compiler_params=_params(("arbitrary",)),
        name="in_proj",
    )(x, mod, g, w)


def _head_norm(xf, g):
    return xf * lax.rsqrt(jnp.mean(xf * xf, axis=-1, keepdims=True) + EPS) * g


def _rope(xf, cos, sin_signed):
    lane = lax.broadcasted_iota(jnp.int32, xf.shape, 1)
    first = (lane % AXIS_DIM) < (AXIS_DIM // 2)
    partner = jnp.where(first, pltpu.roll(xf, HEAD_DIM - AXIS_DIM // 2, 1), pltpu.roll(xf, AXIS_DIM // 2, 1))
    return xf * cos + partner * sin_signed


_NT = (((1,), (1,)), ((), ()))


def _attn_kernel(*refs, tq, has_lat):
    if has_lat:
        (q_ref, kc_ref, vc_ref, kl_ref, vl_ref, cos_ref, sin_ref, gqk_ref, o_ref, kcn, kln) = refs
    else:
        (q_ref, kc_ref, vc_ref, gqk_ref, o_ref, kcn) = refs
    qi = pl.program_id(2)
    gq = gqk_ref[0:1, :]
    gk = gqk_ref[1:2, :]

    @pl.when(qi == 0)
    def _():
        kcn[...] = _head_norm(kc_ref[...].astype(F32), gk).astype(BF16)
        if has_lat:
            k = _head_norm(kl_ref[...].astype(F32), gk)
            kln[...] = _rope(k, cos_ref[...], sin_ref[...]).astype(BF16)

    scale = HEAD_DIM ** -0.5
    heads = []
    for h in range(Q_PER_KV):
        qh = _head_norm(q_ref[:, h * HEAD_DIM:(h + 1) * HEAD_DIM].astype(F32), gq)
        if has_lat:
            rows = pl.ds(pl.multiple_of(qi * tq, tq), tq)
            qh = _rope(qh, cos_ref[rows, :], sin_ref[rows, :])
        heads.append((qh * scale).astype(BF16))
    qs = jnp.concatenate(heads, axis=0)

    s_c = lax.dot_general(qs, kcn[...], _NT, preferred_element_type=F32)
    m = jnp.max(s_c, axis=-1, keepdims=True)
    if has_lat:
        s_l = lax.dot_general(qs, kln[...], _NT, preferred_element_type=F32)
        m = jnp.maximum(m, jnp.max(s_l, axis=-1, keepdims=True))
    p_c = jnp.exp(s_c - m)
    l = jnp.sum(p_c, axis=-1, keepdims=True)
    o = jnp.dot(p_c.astype(BF16), vc_ref[...], preferred_element_type=F32)
    if has_lat:
        p_l = jnp.exp(s_l - m)
        l = l + jnp.sum(p_l, axis=-1, keepdims=True)
        o = o + jnp.dot(p_l.astype(BF16), vl_ref[...], preferred_element_type=F32)
    o = o / l
    for h in range(Q_PER_KV):
        o_ref[:, h * HEAD_DIM:(h + 1) * HEAD_DIM] = o[h * tq:(h + 1) * tq].astype(BF16)


def _attention(zq, zc, gqk, cos, sin_signed, batch, q_len, ctx_len, has_lat, tq):
    nq = q_len // tq
    kv_c = lambda col: pl.BlockSpec((ctx_len, HEAD_DIM), lambda b, k, i: (b, col + k))
    kv_l = lambda col: pl.BlockSpec((q_len, HEAD_DIM), lambda b, k, i: (b, col + k))
    in_specs = [pl.BlockSpec((tq, Q_PER_KV * HEAD_DIM), lambda b, k, i: (b * nq + i, k)),
                kv_c(K_COL128), kv_c(V_COL128)]
    args = [zq, zc, zc]
    scratch = [pltpu.VMEM((ctx_len, HEAD_DIM), BF16)]
    if has_lat:
        in_specs += [kv_l(K_COL128), kv_l(V_COL128),
                     pl.BlockSpec((q_len, HEAD_DIM), lambda b, k, i: (0, 0)),
                     pl.BlockSpec((q_len, HEAD_DIM), lambda b, k, i: (0, 0))]
        args += [zq, zq, cos, sin_signed]
        scratch.append(pltpu.VMEM((q_len, HEAD_DIM), BF16))
    in_specs.append(pl.BlockSpec((2, HEAD_DIM), lambda b, k, i: (0, 0)))
    args.append(gqk)
    return pl.pallas_call(
        functools.partial(_attn_kernel, tq=tq, has_lat=has_lat),
        grid=(batch, N_KV_HEADS, nq),
        in_specs=in_specs,
        out_specs=pl.BlockSpec((tq, Q_PER_KV * HEAD_DIM), lambda b, k, i: (b * nq + i, k)),
        out_shape=jax.ShapeDtypeStruct((batch * q_len, ATTN_WIDTH), BF16),
        scratch_shapes=scratch,
        compiler_params=_params(("arbitrary", "arbitrary", "arbitrary")),
        name="attention_lat" if has_lat else "attention_ctx",
    )(*args)


def _cmlp_kernel(zu_ref, zv_ref, ws_ref, bs_ref, o_ref, *, tm):
    bias = bs_ref[...]
    for n in range(tm // CHUNK):
        rows = slice(n * CHUNK, (n + 1) * CHUNK)
        v = jax.nn.gelu(zv_ref[rows, :].astype(F32)).astype(BF16)
        parts = [jnp.dot(ws_ref[g], v[:, g * 128:(g + 1) * 128], preferred_element_type=F32)
                 for g in range(MLP_GROUPS)]
        s = jnp.concatenate(parts, axis=-1) + bias
        u = jax.nn.gelu(zu_ref[rows, :].astype(F32))
        o_ref[rows, :] = (u * s).astype(BF16)


def _chunk_mlp(z, ws, bias, tm):
    M = z.shape[0]
    return pl.pallas_call(
        functools.partial(_cmlp_kernel, tm=tm),
        grid=(M // tm,),
        in_specs=[
            pl.BlockSpec((tm, MLP_WIDTH), lambda i: (i, ZM_COL512)),
            pl.BlockSpec((tm, MLP_WIDTH), lambda i: (i, ZM_COL512 + 1)),
            pl.BlockSpec((MLP_GROUPS, CHUNK, CHUNK), lambda i: (0, 0, 0)),
            pl.BlockSpec((CHUNK, MLP_WIDTH), lambda i: (0, 0)),
        ],
        out_specs=pl.BlockSpec((tm, MLP_WIDTH), lambda i: (i, 0)),
        out_shape=jax.ShapeDtypeStruct((M, MLP_WIDTH), BF16),
        compiler_params=_params(("arbitrary",)),
        name="chunk_mlp",
    )(z, z, ws, bias)


def _lru_kernel(*refs, T, nchunks, rev, batch):
    if rev:
        (zx_ref, zp_ref, zn_ref, h0_ref, cw_ref, cb_ref, w_ref, b_ref, lam_ref, zg_ref, hf_ref,
         out_ref, hfin_ref, a_scr, bx_scr, hc_scr) = refs
    else:
        (zx_ref, zp_ref, zn_ref, h0_ref, cw_ref, cb_ref, w_ref, b_ref, lam_ref,
         out_ref, hfin_ref, a_scr, bx_scr, hc_scr) = refs
    S = T + LRU_PAD
    s = pl.program_id(0)
    c = (nchunks - 1 - s) if rev else s

    @pl.when(s == 0)
    def _():
        hc_scr[...] = h0_ref[...]

    cw = cw_ref[...]
    cb = cb_ref[...]
    lam = lam_ref[...]
    neg_lam = -lam
    softplus = jnp.maximum(neg_lam, 0.0) + jnp.log1p(jnp.exp(-jnp.abs(neg_lam)))
    coef = -LRU_C * softplus
    row = lax.broadcasted_iota(jnp.int32, (T, LRU_WIDTH), 0)
    keep_prev = jnp.where(c == 0, 0.0, 1.0)
    keep_next = jnp.where(c == nchunks - 1, 0.0, 1.0)

    for b in range(batch):
        x = zx_ref[b].astype(F32)
        prev = zp_ref[b].astype(F32)
        nxt = zn_ref[b].astype(F32)
        pm1 = prev[7:8] * keep_prev
        pm2 = prev[6:7] * keep_prev
        nx0 = nxt[0:1] * keep_next
        xm1 = jnp.where(row == 0, pm1, pltpu.roll(x, 1, 0))
        xm2 = jnp.where(row == 0, pm2, jnp.where(row == 1, pm1, pltpu.roll(x, 2, 0)))
        xp1 = jnp.where(row == T - 1, nx0, pltpu.roll(x, T - 1, 0))
        xr = cb + xm2 * cw[0:1] + xm1 * cw[1:2] + x * cw[2:3] + xp1 * cw[3:4]
        xb = xr.astype(BF16)
        ga = jnp.concatenate(
            [jnp.dot(xb[:, k * 128:(k + 1) * 128], w_ref[0, k], preferred_element_type=F32)
             for k in range(LRU_BLOCKS)], axis=-1)
        gi = jnp.concatenate(
            [jnp.dot(xb[:, k * 128:(k + 1) * 128], w_ref[1, k], preferred_element_type=F32)
             for k in range(LRU_BLOCKS)], axis=-1)
        r = jax.nn.sigmoid(ga + b_ref[0:1, :])
        i = jax.nn.sigmoid(gi + b_ref[1:2, :])
        log_a = coef * r
        t = jnp.tanh(log_a)
        a = jnp.exp(log_a)
        bx = jnp.sqrt(-2.0 * t / (1.0 - t)) * (i * xr)
        for k in range(LRU_BLOCKS):
            a_scr[k, b * S:b * S + T, :] = a[:, k * 128:(k + 1) * 128]
            bx_scr[k, b * S:b * S + T, :] = bx[:, k * 128:(k + 1) * 128]

    def step(n, hs):
        tt = (T - 1 - n) if rev else n
        rows = pl.ds(tt, batch, stride=S)
        out = []
        for k in range(LRU_BLOCKS):
            h = a_scr[k, rows, :] * hs[k] + bx_scr[k, rows, :]
            bx_scr[k, rows, :] = h
            out.append(h)
        return tuple(out)

    h_init = tuple(hc_scr[:, k * 128:(k + 1) * 128] for k in range(LRU_BLOCKS))
    h_last = jnp.concatenate(lax.fori_loop(0, T, step, h_init), axis=-1)
    hc_scr[...] = h_last
    hfin_ref[...] = h_last

    for b in range(batch):
        hb = jnp.concatenate([bx_scr[k, b * S:b * S + T, :] for k in range(LRU_BLOCKS)], axis=-1)
        if rev:
            gate = jax.nn.gelu(zg_ref[b].astype(F32))
            out_ref[b] = ((hf_ref[b] + hb) * gate).astype(BF16)
        else:
            out_ref[b] = hb


def _lru_pass(z3, h0, cw, cb, w, bvec, lam, rev, hf=None):
    batch, L, _ = z3.shape
    T = min(LRU_T, L)
    nchunks = L // T
    nb8 = L // 8
    cidx = (lambda s: nchunks - 1 - s) if rev else (lambda s: s)
    main = lambda col: pl.BlockSpec((batch, T, LRU_WIDTH), lambda s: (0, cidx(s), col))
    prev_map = lambda s: (0, jnp.maximum(cidx(s) * (T // 8) - 1, 0), ZX_COL512)
    next_map = lambda s: (0, jnp.minimum((cidx(s) + 1) * (T // 8), nb8 - 1), ZX_COL512)
    const2 = lambda shape: pl.BlockSpec(shape, lambda s: (0, 0))
    in_specs = [main(ZX_COL512),
                pl.BlockSpec((batch, 8, LRU_WIDTH), prev_map),
                pl.BlockSpec((batch, 8, LRU_WIDTH), next_map),
                const2((batch, LRU_WIDTH)),
                const2((4, LRU_WIDTH)), const2((1, LRU_WIDTH)),
                pl.BlockSpec((2, LRU_BLOCKS, 128, 128), lambda s: (0, 0, 0, 0)),
                const2((2, LRU_WIDTH)), const2((1, LRU_WIDTH))]
    args = [z3, z3, z3, h0, cw, cb, w, bvec, lam]
    if rev:
        in_specs += [main(ZG_COL512), pl.BlockSpec((batch, T, LRU_WIDTH), lambda s: (0, cidx(s), 0))]
        args += [z3, hf]
    out_dtype = BF16 if rev else F32
    S = T + LRU_PAD
    return pl.pallas_call(
        functools.partial(_lru_kernel, T=T, nchunks=nchunks, rev=rev, batch=batch),
        grid=(nchunks,),
        in_specs=in_specs,
        out_specs=[pl.BlockSpec((batch, T, LRU_WIDTH), lambda s: (0, cidx(s), 0)),
                   const2((batch, LRU_WIDTH))],
        out_shape=[jax.ShapeDtypeStruct((batch, L, LRU_WIDTH), out_dtype),
                   jax.ShapeDtypeStruct((batch, LRU_WIDTH), F32)],
        scratch_shapes=[pltpu.VMEM((LRU_BLOCKS, batch * S, 128), F32),
                        pltpu.VMEM((LRU_BLOCKS, batch * S, 128), F32),
                        pltpu.VMEM((batch, LRU_WIDTH), F32)],
        compiler_params=_params(("arbitrary",)),
        name="lru_rev" if rev else "lru_fwd",
    )(*args)


def _out_proj_kernel(a_ref, m_ref, r_ref, x_ref, mod_ref, g_ref, w_ref, o_ref):
    mix = jnp.concatenate([a_ref[...], m_ref[...], r_ref[...]], axis=-1)
    y = jnp.dot(mix, w_ref[...], preferred_element_type=F32)
    o_ref[...] = x_ref[...] + mod_ref[0][2:3] * _rms(y, g_ref[...])


def _out_proj(attn, mlp, lru, x, mod, g, w, rows_per_mod, tm):
    M = x.shape[0]
    if rows_per_mod is None:
        mod_map = lambda i: (8, 0, 0)
    else:
        blocks = rows_per_mod // tm
        mod_map = lambda i: (i // blocks, 0, 0)
    return pl.pallas_call(
        _out_proj_kernel,
        grid=(M // tm,),
        in_specs=[
            pl.BlockSpec((tm, ATTN_WIDTH), lambda i: (i, 0)),
            pl.BlockSpec((tm, MLP_WIDTH), lambda i: (i, 0)),
            pl.BlockSpec((tm, LRU_WIDTH), lambda i: (i, 0)),
            pl.BlockSpec((tm, D_MODEL), lambda i: (i, 0)),
            pl.BlockSpec((1, 6, D_MODEL), mod_map),
            pl.BlockSpec((1, D_MODEL), lambda i: (0, 0)),
            pl.BlockSpec((D_MODEL, D_MODEL), lambda i: (0, 0)),
        ],
        out_specs=pl.BlockSpec((tm, D_MODEL), lambda i: (i, 0)),
        out_shape=jax.ShapeDtypeStruct((M, D_MODEL), F32),
        compiler_params=_params(("arbitrary",)),
        name="out_proj",
    )(attn, mlp, lru, x, mod, g, w)


def _ffn_kernel(x_ref, mod_ref, g2_ref, g3_ref, wg_ref, wu_ref, wo_ref, o_ref, h_scr, acc_scr):
    j = pl.program_id(1)
    m = mod_ref[0]

    @pl.when(j == 0)
    def _():
        h = _rms(x_ref[...], g2_ref[...]) * (1.0 + m[4:5]) + m[3:4]
        h_scr[...] = h.astype(BF16)
        acc_scr[...] = jnp.zeros_like(acc_scr)

    h = h_scr[...]
    gate = jnp.dot(h, wg_ref[...], preferred_element_type=F32)
    up = jnp.dot(h, wu_ref[...], preferred_element_type=F32)
    act = (gate * jax.nn.sigmoid(gate) * up).astype(BF16)
    acc_scr[...] += jnp.dot(act, wo_ref[...], preferred_element_type=F32)

    @pl.when(j == pl.num_programs(1) - 1)
    def _():
        o_ref[...] = x_ref[...] + m[5:6] * _rms(acc_scr[...], g3_ref[...])


def _ffn(x, mod, g2, g3, w_in, w_out, rows_per_mod, tm, tf):
    M = x.shape[0]
    nf = D_FF // tf
    if rows_per_mod is None:
        mod_map = lambda i, j: (8, 0, 0)
    else:
        blocks = rows_per_mod // tm
        mod_map = lambda i, j: (i // blocks, 0, 0)
    return pl.pallas_call(
        _ffn_kernel,
        grid=(M // tm, nf),
        in_specs=[
            pl.BlockSpec((tm, D_MODEL), lambda i, j: (i, 0)),
            pl.BlockSpec((1, 6, D_MODEL), mod_map),
            pl.BlockSpec((1, D_MODEL), lambda i, j: (0, 0)),
            pl.BlockSpec((1, D_MODEL), lambda i, j: (0, 0)),
            pl.BlockSpec((D_MODEL, tf), lambda i, j: (0, j)),
            pl.BlockSpec((D_MODEL, tf), lambda i, j: (0, j + nf)),
            pl.BlockSpec((tf, D_MODEL), lambda i, j: (j, 0)),
        ],
        out_specs=pl.BlockSpec((tm, D_MODEL), lambda i, j: (i, 0)),
        out_shape=jax.ShapeDtypeStruct((M, D_MODEL), F32),
        scratch_shapes=[pltpu.VMEM((tm, D_MODEL), BF16), pltpu.VMEM((tm, D_MODEL), F32)],
        compiler_params=_params(("arbitrary", "arbitrary")),
        name="ffn",
    )(x, mod, g2, g3, w_in, w_in, w_out)


def _rope_tables(n_tokens):
    rows = n_tokens // GRID_W
    row_ids = jnp.repeat(jnp.arange(rows, dtype=F32), GRID_W)
    col_ids = jnp.tile(jnp.arange(GRID_W, dtype=F32), rows)
    inv_freq = ROPE_THETA ** (-jnp.arange(0, AXIS_DIM, 2, dtype=F32) / AXIS_DIM)
    ang_r = row_ids[:, None] * inv_freq
    ang_c = col_ids[:, None] * inv_freq
    ang = jnp.concatenate([ang_r, ang_r, ang_c, ang_c], axis=-1)
    sign = jnp.where((jnp.arange(HEAD_DIM) % AXIS_DIM) < (AXIS_DIM // 2), -1.0, 1.0).astype(F32)
    return jnp.cos(ang), jnp.sin(ang) * sign


def kernel(x, c, ctx, c_ctx, w_mod, b_mod, g_norm, w_in, g_qk, w_s, b_s, conv_w, conv_b,
           lru_w, lru_b, lru_lam, w_out, w_ffn_in, w_ffn_out):
    batch, seq, d = x.shape
    ctx_len = ctx.shape[1]
    depth = w_mod.shape[0]
    cos, sin_signed = _rope_tables(seq)

    c_all = jnp.concatenate([c, c_ctx[None, :], jnp.zeros((MOD_ROWS - batch - 1, d), F32)], axis=0)
    mod_all = _modulation(c_all, w_mod, b_mod).reshape(depth, MOD_ROWS, 6, d)

    xl = x.reshape(batch * seq, d)
    xc = ctx.reshape(batch * ctx_len, d)
    zeros_h = jnp.zeros((batch, LRU_WIDTH), F32)

    for l in range(depth):
        last = l == depth - 1
        mod = mod_all[l]
        g = g_norm[l].reshape(4, 1, d)
        w_in_l = w_in[l].astype(BF16)
        w_out_l = w_out[l].astype(BF16)
        w_fi = w_ffn_in[l].astype(BF16)
        w_fo = w_ffn_out[l].astype(BF16)
        ws = w_s[l].astype(BF16)
        bias = jnp.repeat(b_s[l].T, CHUNK, axis=1)
        lw = lru_w[l].astype(BF16)
        cb = conv_b[l].reshape(1, LRU_WIDTH)

        zl = _in_proj(xl, mod, g[0], w_in_l, seq, 512)
        zc = _in_proj(xc, mod, g[0], w_in_l, None, 512)

        attn = _attention(zl, zc, g_qk[l], cos, sin_signed, batch, seq, ctx_len, True, 128)
        mlp = _chunk_mlp(zl, ws, bias, 512)

        zc3 = zc.reshape(batch, ctx_len, IN_COLS)
        zl3 = zl.reshape(batch, seq, IN_COLS)
        lru_args = lambda dd: (conv_w[l], cb, lw[dd], lru_b[l, dd], lru_lam[l, dd].reshape(1, LRU_WIDTH))
        hcf, h0f = _lru_pass(zc3, zeros_h, *lru_args(0), rev=False)
        lru_c, h0r = _lru_pass(zc3, zeros_h, *lru_args(1), rev=True, hf=hcf)
        hlf, _ = _lru_pass(zl3, h0f, *lru_args(0), rev=False)
        lru_l, _ = _lru_pass(zl3, h0r, *lru_args(1), rev=True, hf=hlf)

        xl = _out_proj(attn, mlp, lru_l.reshape(batch * seq, LRU_WIDTH), xl, mod, g[1], w_out_l, seq, 512)
        xl = _ffn(xl, mod, g[2], g[3], w_fi, w_fo, seq, 512, 512)

        if not last:
            attn_c = _attention(zc, zc, g_qk[l], cos, sin_signed, batch, ctx_len, ctx_len, False, ctx_len)
            mlp_c = _chunk_mlp(zc, ws, bias, 512)
            xc = _out_proj(attn_c, mlp_c, lru_c.reshape(batch * ctx_len, LRU_WIDTH), xc, mod, g[1], w_out_l,
                           None, 512)
            xc = _ffn(xc, mod, g[2], g[3], w_fi, w_fo, None, 512, 512)

    return xl.reshape(batch, seq, d)
```

```python
import functools

import jax
import jax.numpy as jnp
from jax import lax
from jax.experimental import pallas as pl
from jax.experimental.pallas import tpu as pltpu

F32 = jnp.float32
BF16 = jnp.bfloat16

D_MODEL = 2048
GRID_W = 64
EPS = 1e-6
HEAD_DIM = 128
N_Q_HEADS = 8
N_KV_HEADS = 2
Q_PER_KV = 4
ATTN_WIDTH = 1024
KV_WIDTH = 256
ROPE_THETA = 10000.0
AXIS_DIM = 64
CHUNK = 128
MLP_WIDTH = 512
MLP_GROUPS = 4
LRU_WIDTH = 512
LRU_BLOCKS = 4
LRU_C = 8.0
IN_COLS = 3584
D_FF = 5632

K_COL128 = ATTN_WIDTH // 128
V_COL128 = (ATTN_WIDTH + KV_WIDTH) // 128
ZM_COL512 = (ATTN_WIDTH + 2 * KV_WIDTH) // 512
ZX_COL512 = ZM_COL512 + 2
ZG_COL512 = ZX_COL512 + 1

VMEM_LIMIT = 56 * 1024 * 1024
MOD_ROWS = 16
ATTN_KC = 256
LRU_T = 256
LRU_PAD = 8


def _params(sem):
    return pltpu.CompilerParams(dimension_semantics=sem, vmem_limit_bytes=VMEM_LIMIT)


def _rms(xf, g):
    return xf * lax.rsqrt(jnp.mean(xf * xf, axis=-1, keepdims=True) + EPS) * g


def _mod_kernel(c_ref, w_ref, b_ref, o_ref):
    c = c_ref[...]
    a = (c * jax.nn.sigmoid(c)).astype(BF16)
    o_ref[0] = jnp.dot(a, w_ref[0].astype(BF16), preferred_element_type=F32) + b_ref[0]


def _modulation(c_all, w_mod, b_mod):
    depth, d, n = w_mod.shape
    tn = 1024
    return pl.pallas_call(
        _mod_kernel,
        grid=(depth, n // tn),
        in_specs=[
            pl.BlockSpec((MOD_ROWS, d), lambda l, j: (0, 0)),
            pl.BlockSpec((1, d, tn), lambda l, j: (l, 0, j)),
            pl.BlockSpec((1, 1, tn), lambda l, j: (l, 0, j)),
        ],
        out_specs=pl.BlockSpec((1, MOD_ROWS, tn), lambda l, j: (l, 0, j)),
        out_shape=jax.ShapeDtypeStruct((depth, MOD_ROWS, n), F32),
        compiler_params=_params(("arbitrary", "arbitrary")),
        name="modulation",
    )(c_all, w_mod, b_mod.reshape(depth, 1, n))


def _in_proj_kernel(x_ref, mod_ref, g_ref, w_ref, z_ref):
    m = mod_ref[0]
    h = _rms(x_ref[...], g_ref[...]) * (1.0 + m[1:2]) + m[0:1]
    z_ref[...] = jnp.dot(h.astype(BF16), w_ref[...], preferred_element_type=F32).astype(BF16)


def _in_proj(x, mod, g, w, layer, rows_per_mod, tm):
    M = x.shape[0]
    if rows_per_mod is None:
        mod_map = lambda i: (8, 0, 0)
    else:
        blocks = rows_per_mod // tm
        mod_map = lambda i: (i // blocks, 0, 0)
    return pl.pallas_call(
        _in_proj_kernel,
        grid=(M // tm,),
        in_specs=[
            pl.BlockSpec((tm, D_MODEL), lambda i: (i, 0)),
            pl.BlockSpec((1, 6, D_MODEL), mod_map),
            pl.BlockSpec((1, D_MODEL), lambda i: (0, 0)),
            pl.BlockSpec((None, D_MODEL, IN_COLS), lambda i: (layer, 0, 0)),
        ],
        out_specs=pl.BlockSpec((tm, IN_COLS), lambda i: (i, 0)),
        out_shape=jax.ShapeDtypeStruct((M, IN_COLS), BF16),
        compiler_params=_params(("arbitrary",)),
        name="in_proj",
    )(x, mod, g, w)


def _head_norm(xf, g):
    return xf * lax.rsqrt(jnp.mean(xf * xf, axis=-1, keepdims=True) + EPS) * g


def _rope(xf, cos, sin_signed):
    lane = lax.broadcasted_iota(jnp.int32, xf.shape, 1)
    first = (lane % AXIS_DIM) < (AXIS_DIM // 2)
    partner = jnp.where(first, pltpu.roll(xf, HEAD_DIM - AXIS_DIM // 2, 1), pltpu.roll(xf, AXIS_DIM // 2, 1))
    return xf * cos + partner * sin_signed


_NT = (((1,), (1,)), ((), ()))


def _attn_kernel(*refs, tq, ctx_len, has_lat):
    if has_lat:
        (q_ref, kc_ref, vc_ref, kl_ref, vl_ref, cos_ref, sin_ref, gqk_ref, o_ref, kn_scr, vt_scr) = refs
    else:
        (q_ref, kc_ref, vc_ref, gqk_ref, o_ref, kn_scr, vt_scr) = refs
    qi = pl.program_id(2)
    gq = gqk_ref[0:1, :]
    gk = gqk_ref[1:2, :]
    n_keys = kn_scr.shape[0]

    @pl.when(qi == 0)
    def _():
        kn_scr[0:ctx_len, :] = _head_norm(kc_ref[...].astype(F32), gk).astype(BF16)
        vt_scr[:, 0:ctx_len] = vc_ref[...].astype(F32).T.astype(BF16)
        if has_lat:
            k = _head_norm(kl_ref[...].astype(F32), gk)
            kn_scr[ctx_len:, :] = _rope(k, cos_ref[...], sin_ref[...]).astype(BF16)
            vt_scr[:, ctx_len:] = vl_ref[...].astype(F32).T.astype(BF16)

    scale = HEAD_DIM ** -0.5
    heads = []
    for h in range(Q_PER_KV):
        qh = _head_norm(q_ref[:, h * HEAD_DIM:(h + 1) * HEAD_DIM].astype(F32), gq)
        if has_lat:
            rows = pl.ds(pl.multiple_of(qi * tq, tq), tq)
            qh = _rope(qh, cos_ref[rows, :], sin_ref[rows, :])
        heads.append((qh * scale).astype(BF16))
    qs = jnp.concatenate(heads, axis=0)

    s = lax.dot_general(kn_scr[...], qs, _NT, preferred_element_type=F32)
    m = jnp.max(s, axis=0, keepdims=True)
    l = jnp.zeros((1, Q_PER_KV * tq), F32)
    acc = jnp.zeros((HEAD_DIM, Q_PER_KV * tq), F32)
    for c in range(n_keys // ATTN_KC):
        rows = slice(c * ATTN_KC, (c + 1) * ATTN_KC)
        p = jnp.exp(s[rows] - m)
        l = l + jnp.sum(p, axis=0, keepdims=True)
        acc = acc + jnp.dot(vt_scr[:, rows], p.astype(BF16), preferred_element_type=F32)
    o = (acc / l).T
    for h in range(Q_PER_KV):
        o_ref[:, h * HEAD_DIM:(h + 1) * HEAD_DIM] = o[h * tq:(h + 1) * tq].astype(BF16)


def _attention(zq, zc, gqk, cos, sin_signed, batch, q_len, ctx_len, has_lat, tq):
    nq = q_len // tq
    n_keys = ctx_len + (q_len if has_lat else 0)
    kv_c = lambda col: pl.BlockSpec((ctx_len, HEAD_DIM), lambda b, k, i: (b, col + k))
    kv_l = lambda col: pl.BlockSpec((q_len, HEAD_DIM), lambda b, k, i: (b, col + k))
    in_specs = [pl.BlockSpec((tq, Q_PER_KV * HEAD_DIM), lambda b, k, i: (b * nq + i, k)),
                kv_c(K_COL128), kv_c(V_COL128)]
    args = [zq, zc, zc]
    scratch = [pltpu.VMEM((n_keys, HEAD_DIM), BF16), pltpu.VMEM((HEAD_DIM, n_keys), BF16)]
    if has_lat:
        in_specs += [kv_l(K_COL128), kv_l(V_COL128),
                     pl.BlockSpec((q_len, HEAD_DIM), lambda b, k, i: (0, 0)),
                     pl.BlockSpec((q_len, HEAD_DIM), lambda b, k, i: (0, 0))]
        args += [zq, zq, cos, sin_signed]
    in_specs.append(pl.BlockSpec((2, HEAD_DIM), lambda b, k, i: (0, 0)))
    args.append(gqk)
    return pl.pallas_call(
        functools.partial(_attn_kernel, tq=tq, ctx_len=ctx_len, has_lat=has_lat),
        grid=(batch, N_KV_HEADS, nq),
        in_specs=in_specs,
        out_specs=pl.BlockSpec((tq, Q_PER_KV * HEAD_DIM), lambda b, k, i: (b * nq + i, k)),
        out_shape=jax.ShapeDtypeStruct((batch * q_len, ATTN_WIDTH), BF16),
        scratch_shapes=scratch,
        compiler_params=_params(("arbitrary", "arbitrary", "arbitrary")),
        name="attention_lat" if has_lat else "attention_ctx",
    )(*args)


def _cmlp_kernel(zu_ref, zv_ref, ws_ref, bs_ref, o_ref, *, tm):
    bias = bs_ref[...]
    for n in range(tm // CHUNK):
        rows = slice(n * CHUNK, (n + 1) * CHUNK)
        v = jax.nn.gelu(zv_ref[rows, :].astype(F32)).astype(BF16)
        parts = [jnp.dot(ws_ref[g], v[:, g * 128:(g + 1) * 128], preferred_element_type=F32)
                 for g in range(MLP_GROUPS)]
        s = jnp.concatenate(parts, axis=-1) + bias
        u = jax.nn.gelu(zu_ref[rows, :].astype(F32))
        o_ref[rows, :] = (u * s).astype(BF16)


def _chunk_mlp(z, ws, bias, tm):
    M = z.shape[0]
    return pl.pallas_call(
        functools.partial(_cmlp_kernel, tm=tm),
        grid=(M // tm,),
        in_specs=[
            pl.BlockSpec((tm, MLP_WIDTH), lambda i: (i, ZM_COL512)),
            pl.BlockSpec((tm, MLP_WIDTH), lambda i: (i, ZM_COL512 + 1)),
            pl.BlockSpec((MLP_GROUPS, CHUNK, CHUNK), lambda i: (0, 0, 0)),
            pl.BlockSpec((CHUNK, MLP_WIDTH), lambda i: (0, 0)),
        ],
        out_specs=pl.BlockSpec((tm, MLP_WIDTH), lambda i: (i, 0)),
        out_shape=jax.ShapeDtypeStruct((M, MLP_WIDTH), BF16),
        compiler_params=_params(("arbitrary",)),
        name="chunk_mlp",
    )(z, z, ws, bias)


def _lru_kernel(*refs, T, nchunks, rev, batch):
    if rev:
        (zx_ref, zp_ref, zn_ref, h0_ref, cw_ref, cb_ref, w_ref, b_ref, lam_ref, zg_ref, hf_ref,
         out_ref, hfin_ref, a_scr, bx_scr, hc_scr) = refs
    else:
        (zx_ref, zp_ref, zn_ref, h0_ref, cw_ref, cb_ref, w_ref, b_ref, lam_ref,
         out_ref, hfin_ref, a_scr, bx_scr, hc_scr) = refs
    S = T + LRU_PAD
    s = pl.program_id(0)
    c = (nchunks - 1 - s) if rev else s

    @pl.when(s == 0)
    def _():
        hc_scr[...] = h0_ref[...]

    cw = cw_ref[...]
    cb = cb_ref[...]
    lam = lam_ref[...]
    neg_lam = -lam
    softplus = jnp.maximum(neg_lam, 0.0) + jnp.log1p(jnp.exp(-jnp.abs(neg_lam)))
    coef = -LRU_C * softplus
    row = lax.broadcasted_iota(jnp.int32, (T, LRU_WIDTH), 0)
    keep_prev = jnp.where(c == 0, 0.0, 1.0)
    keep_next = jnp.where(c == nchunks - 1, 0.0, 1.0)

    for b in range(batch):
        x = zx_ref[b].astype(F32)
        prev = zp_ref[b].astype(F32)
        nxt = zn_ref[b].astype(F32)
        pm1 = prev[7:8] * keep_prev
        pm2 = prev[6:7] * keep_prev
        nx0 = nxt[0:1] * keep_next
        xm1 = jnp.where(row == 0, pm1, pltpu.roll(x, 1, 0))
        xm2 = jnp.where(row == 0, pm2, jnp.where(row == 1, pm1, pltpu.roll(x, 2, 0)))
        xp1 = jnp.where(row == T - 1, nx0, pltpu.roll(x, T - 1, 0))
        xr = cb + xm2 * cw[0:1] + xm1 * cw[1:2] + x * cw[2:3] + xp1 * cw[3:4]
        xb = xr.astype(BF16)
        ga = jnp.concatenate(
            [jnp.dot(xb[:, k * 128:(k + 1) * 128], w_ref[0, k], preferred_element_type=F32)
             for k in range(LRU_BLOCKS)], axis=-1)
        gi = jnp.concatenate(
            [jnp.dot(xb[:, k * 128:(k + 1) * 128], w_ref[1, k], preferred_element_type=F32)
             for k in range(LRU_BLOCKS)], axis=-1)
        r = jax.nn.sigmoid(ga + b_ref[0:1, :])
        i = jax.nn.sigmoid(gi + b_ref[1:2, :])
        log_a = coef * r
        t = jnp.tanh(log_a)
        a = jnp.exp(log_a)
        bx = jnp.sqrt(-2.0 * t / (1.0 - t)) * (i * xr)
        for k in range(LRU_BLOCKS):
            a_scr[k, b * S:b * S + T, :] = a[:, k * 128:(k + 1) * 128]
            bx_scr[k, b * S:b * S + T, :] = bx[:, k * 128:(k + 1) * 128]

    def step(n, hs):
        tt = (T - 1 - n) if rev else n
        rows = pl.ds(tt, batch, stride=S)
        out = []
        for k in range(LRU_BLOCKS):
            h = a_scr[k, rows, :] * hs[k] + bx_scr[k, rows, :]
            bx_scr[k, rows, :] = h
            out.append(h)
        return tuple(out)

    h_init = tuple(hc_scr[:, k * 128:(k + 1) * 128] for k in range(LRU_BLOCKS))
    h_last = jnp.concatenate(lax.fori_loop(0, T, step, h_init), axis=-1)
    hc_scr[...] = h_last
    hfin_ref[...] = h_last

    for b in range(batch):
        hb = jnp.concatenate([bx_scr[k, b * S:b * S + T, :] for k in range(LRU_BLOCKS)], axis=-1)
        if rev:
            gate = jax.nn.gelu(zg_ref[b].astype(F32))
            out_ref[b] = ((hf_ref[b] + hb) * gate).astype(BF16)
        else:
            out_ref[b] = hb


def _lru_pass(z3, h0, cw, cb, w, bvec, lam, rev, hf=None):
    batch, L, _ = z3.shape
    T = min(LRU_T, L)
    nchunks = L // T
    nb8 = L // 8
    cidx = (lambda s: nchunks - 1 - s) if rev else (lambda s: s)
    main = lambda col: pl.BlockSpec((batch, T, LRU_WIDTH), lambda s: (0, cidx(s), col))
    prev_map = lambda s: (0, jnp.maximum(cidx(s) * (T // 8) - 1, 0), ZX_COL512)
    next_map = lambda s: (0, jnp.minimum((cidx(s) + 1) * (T // 8), nb8 - 1), ZX_COL512)
    const2 = lambda shape: pl.BlockSpec(shape, lambda s: (0, 0))
    in_specs = [main(ZX_COL512),
                pl.BlockSpec((batch, 8, LRU_WIDTH), prev_map),
                pl.BlockSpec((batch, 8, LRU_WIDTH), next_map),
                const2((batch, LRU_WIDTH)),
                const2((4, LRU_WIDTH)), const2((1, LRU_WIDTH)),
                pl.BlockSpec((2, LRU_BLOCKS, 128, 128), lambda s: (0, 0, 0, 0)),
                const2((2, LRU_WIDTH)), const2((1, LRU_WIDTH))]
    args = [z3, z3, z3, h0, cw, cb, w, bvec, lam]
    if rev:
        in_specs += [main(ZG_COL512), pl.BlockSpec((batch, T, LRU_WIDTH), lambda s: (0, cidx(s), 0))]
        args += [z3, hf]
    out_dtype = BF16 if rev else F32
    S = T + LRU_PAD
    return pl.pallas_call(
        functools.partial(_lru_kernel, T=T, nchunks=nchunks, rev=rev, batch=batch),
        grid=(nchunks,),
        in_specs=in_specs,
        out_specs=[pl.BlockSpec((batch, T, LRU_WIDTH), lambda s: (0, cidx(s), 0)),
                   const2((batch, LRU_WIDTH))],
        out_shape=[jax.ShapeDtypeStruct((batch, L, LRU_WIDTH), out_dtype),
                   jax.ShapeDtypeStruct((batch, LRU_WIDTH), F32)],
        scratch_shapes=[pltpu.VMEM((LRU_BLOCKS, batch * S, 128), F32),
                        pltpu.VMEM((LRU_BLOCKS, batch * S, 128), F32),
                        pltpu.VMEM((batch, LRU_WIDTH), F32)],
        compiler_params=_params(("arbitrary",)),
        name="lru_rev" if rev else "lru_fwd",
    )(*args)


def _out_proj_kernel(a_ref, m_ref, r_ref, x_ref, mod_ref, g_ref, w_ref, o_ref):
    mix = jnp.concatenate([a_ref[...], m_ref[...], r_ref[...]], axis=-1)
    y = jnp.dot(mix, w_ref[...], preferred_element_type=F32)
    o_ref[...] = x_ref[...] + mod_ref[0][2:3] * _rms(y, g_ref[...])


def _out_proj(attn, mlp, lru, x, mod, g, w, layer, rows_per_mod, tm):
    M = x.shape[0]
    if rows_per_mod is None:
        mod_map = lambda i: (8, 0, 0)
    else:
        blocks = rows_per_mod // tm
        mod_map = lambda i: (i // blocks, 0, 0)
    return pl.pallas_call(
        _out_proj_kernel,
        grid=(M // tm,),
        in_specs=[
            pl.BlockSpec((tm, ATTN_WIDTH), lambda i: (i, 0)),
            pl.BlockSpec((tm, MLP_WIDTH), lambda i: (i, 0)),
            pl.BlockSpec((tm, LRU_WIDTH), lambda i: (i, 0)),
            pl.BlockSpec((tm, D_MODEL), lambda i: (i, 0)),
            pl.BlockSpec((1, 6, D_MODEL), mod_map),
            pl.BlockSpec((1, D_MODEL), lambda i: (0, 0)),
            pl.BlockSpec((None, D_MODEL, D_MODEL), lambda i: (layer, 0, 0)),
        ],
        out_specs=pl.BlockSpec((tm, D_MODEL), lambda i: (i, 0)),
        out_shape=jax.ShapeDtypeStruct((M, D_MODEL), F32),
        compiler_params=_params(("arbitrary",)),
        name="out_proj",
    )(attn, mlp, lru, x, mod, g, w)


def _ffn_kernel(x_ref, mod_ref, g2_ref, g3_ref, wg_ref, wu_ref, wo_ref, o_ref, h_scr, acc_scr):
    j = pl.program_id(1)
    m = mod_ref[0]

    @pl.when(j == 0)
    def _():
        h = _rms(x_ref[...], g2_ref[...]) * (1.0 + m[4:5]) + m[3:4]
        h_scr[...] = h.astype(BF16)
        acc_scr[...] = jnp.zeros_like(acc_scr)

    h = h_scr[...]
    gate = jnp.dot(h, wg_ref[...], preferred_element_type=F32)
    up = jnp.dot(h, wu_ref[...], preferred_element_type=F32)
    act = (gate * jax.nn.sigmoid(gate) * up).astype(BF16)
    acc_scr[...] += jnp.dot(act, wo_ref[...], preferred_element_type=F32)

    @pl.when(j == pl.num_programs(1) - 1)
    def _():
        o_ref[...] = x_ref[...] + m[5:6] * _rms(acc_scr[...], g3_ref[...])


def _ffn(x, mod, g2, g3, w_in, w_out, layer, rows_per_mod, tm, tf):
    M = x.shape[0]
    nf = D_FF // tf
    if rows_per_mod is None:
        mod_map = lambda i, j: (8, 0, 0)
    else:
        blocks = rows_per_mod // tm
        mod_map = lambda i, j: (i // blocks, 0, 0)
    return pl.pallas_call(
        _ffn_kernel,
        grid=(M // tm, nf),
        in_specs=[
            pl.BlockSpec((tm, D_MODEL), lambda i, j: (i, 0)),
            pl.BlockSpec((1, 6, D_MODEL), mod_map),
            pl.BlockSpec((1, D_MODEL), lambda i, j: (0, 0)),
            pl.BlockSpec((1, D_MODEL), lambda i, j: (0, 0)),
            pl.BlockSpec((None, D_MODEL, tf), lambda i, j: (layer, 0, j)),
            pl.BlockSpec((None, D_MODEL, tf), lambda i, j: (layer, 0, j + nf)),
            pl.BlockSpec((None, tf, D_MODEL), lambda i, j: (layer, j, 0)),
        ],
        out_specs=pl.BlockSpec((tm, D_MODEL), lambda i, j: (i, 0)),
        out_shape=jax.ShapeDtypeStruct((M, D_MODEL), F32),
        scratch_shapes=[pltpu.VMEM((tm, D_MODEL), BF16), pltpu.VMEM((tm, D_MODEL), F32)],
        compiler_params=_params(("arbitrary", "arbitrary")),
        name="ffn",
    )(x, mod, g2, g3, w_in, w_in, w_out)


def _rope_tables(n_tokens):
    rows = n_tokens // GRID_W
    row_ids = jnp.repeat(jnp.arange(rows, dtype=F32), GRID_W)
    col_ids = jnp.tile(jnp.arange(GRID_W, dtype=F32), rows)
    inv_freq = ROPE_THETA ** (-jnp.arange(0, AXIS_DIM, 2, dtype=F32) / AXIS_DIM)
    ang_r = row_ids[:, None] * inv_freq
    ang_c = col_ids[:, None] * inv_freq
    ang = jnp.concatenate([ang_r, ang_r, ang_c, ang_c], axis=-1)
    sign = jnp.where((jnp.arange(HEAD_DIM) % AXIS_DIM) < (AXIS_DIM // 2), -1.0, 1.0).astype(F32)
    return jnp.cos(ang), jnp.sin(ang) * sign


def kernel(x, c, ctx, c_ctx, w_mod, b_mod, g_norm, w_in, g_qk, w_s, b_s, conv_w, conv_b,
           lru_w, lru_b, lru_lam, w_out, w_ffn_in, w_ffn_out):
    batch, seq, d = x.shape
    ctx_len = ctx.shape[1]
    depth = w_mod.shape[0]
    cos, sin_signed = _rope_tables(seq)

    c_all = jnp.concatenate([c, c_ctx[None, :], jnp.zeros((MOD_ROWS - batch - 1, d), F32)], axis=0)
    mod_all = _modulation(c_all, w_mod, b_mod).reshape(depth, MOD_ROWS, 6, d)

    xl = x.reshape(batch * seq, d)
    xc = ctx.reshape(batch * ctx_len, d)
    zeros_h = jnp.zeros((batch, LRU_WIDTH), F32)
    w_in_b = w_in.astype(BF16)
    w_out_b = w_out.astype(BF16)
    w_fi_b = w_ffn_in.astype(BF16)
    w_fo_b = w_ffn_out.astype(BF16)

    for l in range(depth):
        last = l == depth - 1
        mod = mod_all[l]
        g = g_norm[l].reshape(4, 1, d)
        ws = w_s[l].astype(BF16)
        bias = jnp.repeat(b_s[l].T, CHUNK, axis=1)
        lw = lru_w[l].astype(BF16)
        cb = conv_b[l].reshape(1, LRU_WIDTH)

        zl = _in_proj(xl, mod, g[0], w_in_b, l, seq, 512)
        zc = _in_proj(xc, mod, g[0], w_in_b, l, None, 512)

        attn = _attention(zl, zc, g_qk[l], cos, sin_signed, batch, seq, ctx_len, True, 256)
        mlp = _chunk_mlp(zl, ws, bias, 512)

        zc3 = zc.reshape(batch, ctx_len, IN_COLS)
        zl3 = zl.reshape(batch, seq, IN_COLS)
        lru_args = lambda dd: (conv_w[l], cb, lw[dd], lru_b[l, dd], lru_lam[l, dd].reshape(1, LRU_WIDTH))
        hcf, h0f = _lru_pass(zc3, zeros_h, *lru_args(0), rev=False)
        lru_c, h0r = _lru_pass(zc3, zeros_h, *lru_args(1), rev=True, hf=hcf)
        hlf, _ = _lru_pass(zl3, h0f, *lru_args(0), rev=False)
        lru_l, _ = _lru_pass(zl3, h0r, *lru_args(1), rev=True, hf=hlf)

        xl = _out_proj(attn, mlp, lru_l.reshape(batch * seq, LRU_WIDTH), xl, mod, g[1], w_out_b, l, seq, 512)
        xl = _ffn(xl, mod, g[2], g[3], w_fi_b, w_fo_b, l, seq, 512, 512)

        if not last:
            attn_c = _attention(zc, zc, g_qk[l], cos, sin_signed, batch, ctx_len, ctx_len, False, ctx_len)
            mlp_c = _chunk_mlp(zc, ws, bias, 512)
            xc = _out_proj(attn_c, mlp_c, lru_c.reshape(batch * ctx_len, LRU_WIDTH), xc, mod, g[1], w_out_b,
                           l, None, 512)
            xc = _ffn(xc, mod, g[2], g[3], w_fi_b, w_fo_b, l, None, 512, 512)

    return xl.reshape(batch, seq, d)
```

```python
import functools

import jax
import jax.numpy as jnp
from jax import lax
from jax.experimental import pallas as pl
from jax.experimental.pallas import tpu as pltpu

F32 = jnp.float32
BF16 = jnp.bfloat16

D_MODEL = 2048
GRID_W = 64
EPS = 1e-6
HEAD_DIM = 128
N_Q_HEADS = 8
N_KV_HEADS = 2
Q_PER_KV = 4
ATTN_WIDTH = 1024
KV_WIDTH = 256
ROPE_THETA = 10000.0
AXIS_DIM = 64
CHUNK = 128
MLP_WIDTH = 512
MLP_GROUPS = 4
LRU_WIDTH = 512
LRU_BLOCKS = 4
LRU_C = 8.0
IN_COLS = 3584
D_FF = 5632

K_COL128 = ATTN_WIDTH // 128
V_COL128 = (ATTN_WIDTH + KV_WIDTH) // 128
ZM_COL512 = (ATTN_WIDTH + 2 * KV_WIDTH) // 512
ZX_COL512 = ZM_COL512 + 2
ZG_COL512 = ZX_COL512 + 1

VMEM_LIMIT = 56 * 1024 * 1024
MOD_ROWS = 16
ATTN_KC = 256
LOG2_E = 1.4426950408889634
LRU_T = 256
LRU_PAD = 8


def _params(sem):
    return pltpu.CompilerParams(dimension_semantics=sem, vmem_limit_bytes=VMEM_LIMIT)


def _unit_rms(xf):
    return xf * lax.rsqrt(jnp.mean(xf * xf, axis=-1, keepdims=True) + EPS)


def _norm_modulate(xf, g, shift, scale):
    return _unit_rms(xf) * (g * (1.0 + scale)) + shift


def _gated_norm_residual(x, y, g, gate):
    return x + _unit_rms(y) * (g * gate)


def _mod_kernel(c_ref, w_ref, b_ref, o_ref):
    c = c_ref[...]
    a = (c * jax.nn.sigmoid(c)).astype(BF16)
    o_ref[0] = jnp.dot(a, w_ref[0].astype(BF16), preferred_element_type=F32) + b_ref[0]


def _modulation(c_all, w_mod, b_mod):
    depth, d, n = w_mod.shape
    tn = 1024
    return pl.pallas_call(
        _mod_kernel,
        grid=(depth, n // tn),
        in_specs=[
            pl.BlockSpec((MOD_ROWS, d), lambda l, j: (0, 0)),
            pl.BlockSpec((1, d, tn), lambda l, j: (l, 0, j)),
            pl.BlockSpec((1, 1, tn), lambda l, j: (l, 0, j)),
        ],
        out_specs=pl.BlockSpec((1, MOD_ROWS, tn), lambda l, j: (l, 0, j)),
        out_shape=jax.ShapeDtypeStruct((depth, MOD_ROWS, n), F32),
        compiler_params=_params(("arbitrary", "arbitrary")),
        name="modulation",
    )(c_all, w_mod, b_mod.reshape(depth, 1, n))


def _in_proj_kernel(x_ref, mod_ref, g_ref, w_ref, z_ref):
    m = mod_ref[0]
    h = _norm_modulate(x_ref[...], g_ref[...], m[0:1], m[1:2])
    z_ref[...] = jnp.dot(h.astype(BF16), w_ref[...], preferred_element_type=F32).astype(BF16)


def _in_proj(x, mod, g, w, layer, rows_per_mod, tm):
    M = x.shape[0]
    if rows_per_mod is None:
        mod_map = lambda i: (8, 0, 0)
    else:
        blocks = rows_per_mod // tm
        mod_map = lambda i: (i // blocks, 0, 0)
    return pl.pallas_call(
        _in_proj_kernel,
        grid=(M // tm,),
        in_specs=[
            pl.BlockSpec((tm, D_MODEL), lambda i: (i, 0)),
            pl.BlockSpec((1, 6, D_MODEL), mod_map),
            pl.BlockSpec((1, D_MODEL), lambda i: (0, 0)),
            pl.BlockSpec((None, D_MODEL, IN_COLS), lambda i: (layer, 0, 0)),
        ],
        out_specs=pl.BlockSpec((tm, IN_COLS), lambda i: (i, 0)),
        out_shape=jax.ShapeDtypeStruct((M, IN_COLS), BF16),
        compiler_params=_params(("arbitrary",)),
        name="in_proj",
    )(x, mod, g, w)


def _head_norm(xf, g):
    return xf * lax.rsqrt(jnp.mean(xf * xf, axis=-1, keepdims=True) + EPS) * g


def _rope(xf, cos, sin_signed):
    lane = lax.broadcasted_iota(jnp.int32, xf.shape, 1)
    first = (lane % AXIS_DIM) < (AXIS_DIM // 2)
    partner = jnp.where(first, pltpu.roll(xf, HEAD_DIM - AXIS_DIM // 2, 1), pltpu.roll(xf, AXIS_DIM // 2, 1))
    return xf * cos + partner * sin_signed


_NT = (((1,), (1,)), ((), ()))


def _attn_kernel(*refs, tq, ctx_len, has_lat):
    if has_lat:
        (q_ref, kc_ref, vc_ref, kl_ref, vl_ref, cos_ref, sin_ref, gqk_ref, o_ref, kn_scr, vt_scr) = refs
    else:
        (q_ref, kc_ref, vc_ref, gqk_ref, o_ref, kn_scr, vt_scr) = refs
    qi = pl.program_id(2)
    gq = gqk_ref[0:1, :]
    gk = gqk_ref[1:2, :]
    n_keys = kn_scr.shape[0]

    @pl.when(qi == 0)
    def _():
        kn_scr[0:ctx_len, :] = _head_norm(kc_ref[...].astype(F32), gk).astype(BF16)
        vt_scr[:, 0:ctx_len] = vc_ref[...].astype(F32).T.astype(BF16)
        if has_lat:
            k = _head_norm(kl_ref[...].astype(F32), gk)
            kn_scr[ctx_len:, :] = _rope(k, cos_ref[...], sin_ref[...]).astype(BF16)
            vt_scr[:, ctx_len:] = vl_ref[...].astype(F32).T.astype(BF16)

    scale = HEAD_DIM ** -0.5 * LOG2_E
    heads = []
    for h in range(Q_PER_KV):
        qh = _head_norm(q_ref[:, h * HEAD_DIM:(h + 1) * HEAD_DIM].astype(F32), gq)
        if has_lat:
            rows = pl.ds(pl.multiple_of(qi * tq, tq), tq)
            qh = _rope(qh, cos_ref[rows, :], sin_ref[rows, :])
        heads.append((qh * scale).astype(BF16))
    qs = jnp.concatenate(heads, axis=0)

    s = lax.dot_general(kn_scr[...], qs, _NT, preferred_element_type=F32)
    m = jnp.max(s, axis=0, keepdims=True)
    l = jnp.zeros((1, Q_PER_KV * tq), F32)
    acc = jnp.zeros((HEAD_DIM, Q_PER_KV * tq), F32)
    for c in range(n_keys // ATTN_KC):
        rows = slice(c * ATTN_KC, (c + 1) * ATTN_KC)
        p = jnp.exp2(s[rows] - m)
        l = l + jnp.sum(p, axis=0, keepdims=True)
        acc = acc + jnp.dot(vt_scr[:, rows], p.astype(BF16), preferred_element_type=F32)
    o = (acc / l).T
    for h in range(Q_PER_KV):
        o_ref[:, h * HEAD_DIM:(h + 1) * HEAD_DIM] = o[h * tq:(h + 1) * tq].astype(BF16)


def _attention(zq, zc, gqk, cos, sin_signed, batch, q_len, ctx_len, has_lat, tq):
    nq = q_len // tq
    n_keys = ctx_len + (q_len if has_lat else 0)
    kv_c = lambda col: pl.BlockSpec((ctx_len, HEAD_DIM), lambda b, k, i: (b, col + k))
    kv_l = lambda col: pl.BlockSpec((q_len, HEAD_DIM), lambda b, k, i: (b, col + k))
    in_specs = [pl.BlockSpec((tq, Q_PER_KV * HEAD_DIM), lambda b, k, i: (b * nq + i, k)),
                kv_c(K_COL128), kv_c(V_COL128)]
    args = [zq, zc, zc]
    scratch = [pltpu.VMEM((n_keys, HEAD_DIM), BF16), pltpu.VMEM((HEAD_DIM, n_keys), BF16)]
    if has_lat:
        in_specs += [kv_l(K_COL128), kv_l(V_COL128),
                     pl.BlockSpec((q_len, HEAD_DIM), lambda b, k, i: (0, 0)),
                     pl.BlockSpec((q_len, HEAD_DIM), lambda b, k, i: (0, 0))]
        args += [zq, zq, cos, sin_signed]
    in_specs.append(pl.BlockSpec((2, HEAD_DIM), lambda b, k, i: (0, 0)))
    args.append(gqk)
    return pl.pallas_call(
        functools.partial(_attn_kernel, tq=tq, ctx_len=ctx_len, has_lat=has_lat),
        grid=(batch, N_KV_HEADS, nq),
        in_specs=in_specs,
        out_specs=pl.BlockSpec((tq, Q_PER_KV * HEAD_DIM), lambda b, k, i: (b * nq + i, k)),
        out_shape=jax.ShapeDtypeStruct((batch * q_len, ATTN_WIDTH), BF16),
        scratch_shapes=scratch,
        compiler_params=_params(("arbitrary", "arbitrary", "arbitrary")),
        name="attention_lat" if has_lat else "attention_ctx",
    )(*args)


def _cmlp_kernel(zu_ref, zv_ref, ws_ref, bs_ref, o_ref, *, tm):
    bias = bs_ref[...]
    for n in range(tm // CHUNK):
        rows = slice(n * CHUNK, (n + 1) * CHUNK)
        v = jax.nn.gelu(zv_ref[rows, :].astype(F32)).astype(BF16)
        parts = [jnp.dot(ws_ref[g], v[:, g * 128:(g + 1) * 128], preferred_element_type=F32)
                 for g in range(MLP_GROUPS)]
        s = jnp.concatenate(parts, axis=-1) + bias
        u = jax.nn.gelu(zu_ref[rows, :].astype(F32))
        o_ref[rows, :] = (u * s).astype(BF16)


def _chunk_mlp(z, ws, bias, tm):
    M = z.shape[0]
    return pl.pallas_call(
        functools.partial(_cmlp_kernel, tm=tm),
        grid=(M // tm,),
        in_specs=[
            pl.BlockSpec((tm, MLP_WIDTH), lambda i: (i, ZM_COL512)),
            pl.BlockSpec((tm, MLP_WIDTH), lambda i: (i, ZM_COL512 + 1)),
            pl.BlockSpec((MLP_GROUPS, CHUNK, CHUNK), lambda i: (0, 0, 0)),
            pl.BlockSpec((CHUNK, MLP_WIDTH), lambda i: (0, 0)),
        ],
        out_specs=pl.BlockSpec((tm, MLP_WIDTH), lambda i: (i, 0)),
        out_shape=jax.ShapeDtypeStruct((M, MLP_WIDTH), BF16),
        compiler_params=_params(("arbitrary",)),
        name="chunk_mlp",
    )(z, z, ws, bias)


def _lru_kernel(*refs, T, nchunks, rev, batch):
    if rev:
        (zx_ref, zp_ref, zn_ref, h0_ref, cw_ref, cb_ref, w_ref, b_ref, lam_ref, zg_ref, hf_ref,
         out_ref, hfin_ref, a_scr, bx_scr, hc_scr) = refs
    else:
        (zx_ref, zp_ref, zn_ref, h0_ref, cw_ref, cb_ref, w_ref, b_ref, lam_ref,
         out_ref, hfin_ref, a_scr, bx_scr, hc_scr) = refs
    S = T + LRU_PAD
    s = pl.program_id(0)
    c = (nchunks - 1 - s) if rev else s

    @pl.when(s == 0)
    def _():
        hc_scr[...] = h0_ref[...]

    cw = cw_ref[...]
    cb = cb_ref[...]
    lam = lam_ref[...]
    neg_lam = -lam
    softplus = jnp.maximum(neg_lam, 0.0) + jnp.log1p(jnp.exp(-jnp.abs(neg_lam)))
    half_coef = (-0.5 * LRU_C) * softplus
    half_ba = 0.5 * b_ref[0:1, :]
    half_bi = 0.5 * b_ref[1:2, :]
    keep_prev = jnp.where(c == 0, 0.0, 1.0)
    keep_next = jnp.where(c == nchunks - 1, 0.0, 1.0)
    TE = T + 16

    for b in range(batch):
        x = zx_ref[b].astype(F32)
        xe = jnp.concatenate([zp_ref[b].astype(F32) * keep_prev, x, zn_ref[b].astype(F32) * keep_next], axis=0)
        xm2 = pltpu.roll(xe, 2, 0)[8:T + 8]
        xm1 = pltpu.roll(xe, 1, 0)[8:T + 8]
        xp1 = pltpu.roll(xe, TE - 1, 0)[8:T + 8]
        xr = cb + xm2 * cw[0:1] + xm1 * cw[1:2] + x * cw[2:3] + xp1 * cw[3:4]
        xb = xr.astype(BF16)
        ga = jnp.concatenate(
            [jnp.dot(xb[:, k * 128:(k + 1) * 128], w_ref[0, k], preferred_element_type=F32)
             for k in range(LRU_BLOCKS)], axis=-1)
        gi = jnp.concatenate(
            [jnp.dot(xb[:, k * 128:(k + 1) * 128], w_ref[1, k], preferred_element_type=F32)
             for k in range(LRU_BLOCKS)], axis=-1)
        log_a = half_coef * (1.0 + jnp.tanh(0.5 * ga + half_ba))
        i = 0.5 * (1.0 + jnp.tanh(0.5 * gi + half_bi))
        a = jnp.exp(log_a)
        y = 1.0 - a * a
        bx = jnp.where(y > 0.0, y * lax.rsqrt(y), 0.0) * (i * xr)
        for k in range(LRU_BLOCKS):
            a_scr[k, b * S:b * S + T, :] = a[:, k * 128:(k + 1) * 128]
            bx_scr[k, b * S:b * S + T, :] = bx[:, k * 128:(k + 1) * 128]

    def step(n, hs):
        tt = (T - 1 - n) if rev else n
        rows = pl.ds(tt, batch, stride=S)
        out = []
        for k in range(LRU_BLOCKS):
            h = a_scr[k, rows, :] * hs[k] + bx_scr[k, rows, :]
            bx_scr[k, rows, :] = h
            out.append(h)
        return tuple(out)

    h_init = tuple(hc_scr[:, k * 128:(k + 1) * 128] for k in range(LRU_BLOCKS))
    h_last = jnp.concatenate(lax.fori_loop(0, T, step, h_init, unroll=4), axis=-1)
    hc_scr[...] = h_last
    hfin_ref[...] = h_last

    for b in range(batch):
        hb = jnp.concatenate([bx_scr[k, b * S:b * S + T, :] for k in range(LRU_BLOCKS)], axis=-1)
        if rev:
            gate = jax.nn.gelu(zg_ref[b].astype(F32))
            out_ref[b] = ((hf_ref[b] + hb) * gate).astype(BF16)
        else:
            out_ref[b] = hb


def _lru_pass(z3, h0, cw, cb, w, bvec, lam, rev, hf=None):
    batch, L, _ = z3.shape
    T = min(LRU_T, L)
    nchunks = L // T
    nb8 = L // 8
    cidx = (lambda s: nchunks - 1 - s) if rev else (lambda s: s)
    main = lambda col: pl.BlockSpec((batch, T, LRU_WIDTH), lambda s: (0, cidx(s), col))
    prev_map = lambda s: (0, jnp.maximum(cidx(s) * (T // 8) - 1, 0), ZX_COL512)
    next_map = lambda s: (0, jnp.minimum((cidx(s) + 1) * (T // 8), nb8 - 1), ZX_COL512)
    const2 = lambda shape: pl.BlockSpec(shape, lambda s: (0, 0))
    in_specs = [main(ZX_COL512),
                pl.BlockSpec((batch, 8, LRU_WIDTH), prev_map),
                pl.BlockSpec((batch, 8, LRU_WIDTH), next_map),
                const2((batch, LRU_WIDTH)),
                const2((4, LRU_WIDTH)), const2((1, LRU_WIDTH)),
                pl.BlockSpec((2, LRU_BLOCKS, 128, 128), lambda s: (0, 0, 0, 0)),
                const2((2, LRU_WIDTH)), const2((1, LRU_WIDTH))]
    args = [z3, z3, z3, h0, cw, cb, w, bvec, lam]
    if rev:
        in_specs += [main(ZG_COL512), pl.BlockSpec((batch, T, LRU_WIDTH), lambda s: (0, cidx(s), 0))]
        args += [z3, hf]
    out_dtype = BF16 if rev else F32
    S = T + LRU_PAD
    return pl.pallas_call(
        functools.partial(_lru_kernel, T=T, nchunks=nchunks, rev=rev, batch=batch),
        grid=(nchunks,),
        in_specs=in_specs,
        out_specs=[pl.BlockSpec((batch, T, LRU_WIDTH), lambda s: (0, cidx(s), 0)),
                   const2((batch, LRU_WIDTH))],
        out_shape=[jax.ShapeDtypeStruct((batch, L, LRU_WIDTH), out_dtype),
                   jax.ShapeDtypeStruct((batch, LRU_WIDTH), F32)],
        scratch_shapes=[pltpu.VMEM((LRU_BLOCKS, batch * S, 128), F32),
                        pltpu.VMEM((LRU_BLOCKS, batch * S, 128), F32),
                        pltpu.VMEM((batch, LRU_WIDTH), F32)],
        compiler_params=_params(("arbitrary",)),
        name="lru_rev" if rev else "lru_fwd",
    )(*args)


def _out_proj_kernel(a_ref, m_ref, r_ref, x_ref, mod_ref, g_ref, w_ref, o_ref):
    mix = jnp.concatenate([a_ref[...], m_ref[...], r_ref[...]], axis=-1)
    y = jnp.dot(mix, w_ref[...], preferred_element_type=F32)
    o_ref[...] = _gated_norm_residual(x_ref[...], y, g_ref[...], mod_ref[0][2:3])


def _out_proj(attn, mlp, lru, x, mod, g, w, layer, rows_per_mod, tm):
    M = x.shape[0]
    if rows_per_mod is None:
        mod_map = lambda i: (8, 0, 0)
    else:
        blocks = rows_per_mod // tm
        mod_map = lambda i: (i // blocks, 0, 0)
    return pl.pallas_call(
        _out_proj_kernel,
        grid=(M // tm,),
        in_specs=[
            pl.BlockSpec((tm, ATTN_WIDTH), lambda i: (i, 0)),
            pl.BlockSpec((tm, MLP_WIDTH), lambda i: (i, 0)),
            pl.BlockSpec((tm, LRU_WIDTH), lambda i: (i, 0)),
            pl.BlockSpec((tm, D_MODEL), lambda i: (i, 0)),
            pl.BlockSpec((1, 6, D_MODEL), mod_map),
            pl.BlockSpec((1, D_MODEL), lambda i: (0, 0)),
            pl.BlockSpec((None, D_MODEL, D_MODEL), lambda i: (layer, 0, 0)),
        ],
        out_specs=pl.BlockSpec((tm, D_MODEL), lambda i: (i, 0)),
        out_shape=jax.ShapeDtypeStruct((M, D_MODEL), F32),
        compiler_params=_params(("arbitrary",)),
        name="out_proj",
    )(attn, mlp, lru, x, mod, g, w)


def _ffn_kernel(x_ref, mod_ref, g2_ref, g3_ref, wg_ref, wu_ref, wo_ref, o_ref, h_scr, acc_scr):
    j = pl.program_id(1)
    m = mod_ref[0]

    @pl.when(j == 0)
    def _():
        h = _norm_modulate(x_ref[...], g2_ref[...], m[3:4], m[4:5])
        h_scr[...] = h.astype(BF16)
        acc_scr[...] = jnp.zeros_like(acc_scr)

    h = h_scr[...]
    gate = jnp.dot(h, wg_ref[...], preferred_element_type=F32)
    up = jnp.dot(h, wu_ref[...], preferred_element_type=F32)
    act = (gate * jax.nn.sigmoid(gate) * up).astype(BF16)
    acc_scr[...] += jnp.dot(act, wo_ref[...], preferred_element_type=F32)

    @pl.when(j == pl.num_programs(1) - 1)
    def _():
        o_ref[...] = _gated_norm_residual(x_ref[...], acc_scr[...], g3_ref[...], m[5:6])


def _ffn(x, mod, g2, g3, w_in, w_out, layer, rows_per_mod, tm, tf):
    M = x.shape[0]
    nf = D_FF // tf
    if rows_per_mod is None:
        mod_map = lambda i, j: (8, 0, 0)
    else:
        blocks = rows_per_mod // tm
        mod_map = lambda i, j: (i // blocks, 0, 0)
    return pl.pallas_call(
        _ffn_kernel,
        grid=(M // tm, nf),
        in_specs=[
            pl.BlockSpec((tm, D_MODEL), lambda i, j: (i, 0)),
            pl.BlockSpec((1, 6, D_MODEL), mod_map),
            pl.BlockSpec((1, D_MODEL), lambda i, j: (0, 0)),
            pl.BlockSpec((1, D_MODEL), lambda i, j: (0, 0)),
            pl.BlockSpec((None, D_MODEL, tf), lambda i, j: (layer, 0, j)),
            pl.BlockSpec((None, D_MODEL, tf), lambda i, j: (layer, 0, j + nf)),
            pl.BlockSpec((None, tf, D_MODEL), lambda i, j: (layer, j, 0)),
        ],
        out_specs=pl.BlockSpec((tm, D_MODEL), lambda i, j: (i, 0)),
        out_shape=jax.ShapeDtypeStruct((M, D_MODEL), F32),
        scratch_shapes=[pltpu.VMEM((tm, D_MODEL), BF16), pltpu.VMEM((tm, D_MODEL), F32)],
        compiler_params=_params(("arbitrary", "arbitrary")),
        name="ffn",
    )(x, mod, g2, g3, w_in, w_in, w_out)


def _rope_tables(n_tokens):
    rows = n_tokens // GRID_W
    row_ids = jnp.repeat(jnp.arange(rows, dtype=F32), GRID_W)
    col_ids = jnp.tile(jnp.arange(GRID_W, dtype=F32), rows)
    inv_freq = ROPE_THETA ** (-jnp.arange(0, AXIS_DIM, 2, dtype=F32) / AXIS_DIM)
    ang_r = row_ids[:, None] * inv_freq
    ang_c = col_ids[:, None] * inv_freq
    ang = jnp.concatenate([ang_r, ang_r, ang_c, ang_c], axis=-1)
    sign = jnp.where((jnp.arange(HEAD_DIM) % AXIS_DIM) < (AXIS_DIM // 2), -1.0, 1.0).astype(F32)
    return jnp.cos(ang), jnp.sin(ang) * sign


def kernel(x, c, ctx, c_ctx, w_mod, b_mod, g_norm, w_in, g_qk, w_s, b_s, conv_w, conv_b,
           lru_w, lru_b, lru_lam, w_out, w_ffn_in, w_ffn_out):
    batch, seq, d = x.shape
    ctx_len = ctx.shape[1]
    depth = w_mod.shape[0]
    cos, sin_signed = _rope_tables(seq)

    c_all = jnp.concatenate([c, c_ctx[None, :], jnp.zeros((MOD_ROWS - batch - 1, d), F32)], axis=0)
    mod_all = _modulation(c_all, w_mod, b_mod).reshape(depth, MOD_ROWS, 6, d)

    xl = x.reshape(batch * seq, d)
    xc = ctx.reshape(batch * ctx_len, d)
    zeros_h = jnp.zeros((batch, LRU_WIDTH), F32)
    w_in_b = w_in.astype(BF16)
    w_out_b = w_out.astype(BF16)
    w_fi_b = w_ffn_in.astype(BF16)
    w_fo_b = w_ffn_out.astype(BF16)

    for l in range(depth):
        last = l == depth - 1
        mod = mod_all[l]
        g = g_norm[l].reshape(4, 1, d)
        ws = w_s[l].astype(BF16)
        bias = jnp.repeat(b_s[l].T, CHUNK, axis=1)
        lw = lru_w[l].astype(BF16)
        cb = conv_b[l].reshape(1, LRU_WIDTH)

        zl = _in_proj(xl, mod, g[0], w_in_b, l, seq, 512)
        zc = _in_proj(xc, mod, g[0], w_in_b, l, None, 512)

        attn = _attention(zl, zc, g_qk[l], cos, sin_signed, batch, seq, ctx_len, True, 256)
        mlp = _chunk_mlp(zl, ws, bias, 512)

        zc3 = zc.reshape(batch, ctx_len, IN_COLS)
        zl3 = zl.reshape(batch, seq, IN_COLS)
        lru_args = lambda dd: (conv_w[l], cb, lw[dd], lru_b[l, dd], lru_lam[l, dd].reshape(1, LRU_WIDTH))
        hcf, h0f = _lru_pass(zc3, zeros_h, *lru_args(0), rev=False)
        lru_c, h0r = _lru_pass(zc3, zeros_h, *lru_args(1), rev=True, hf=hcf)
        hlf, _ = _lru_pass(zl3, h0f, *lru_args(0), rev=False)
        lru_l, _ = _lru_pass(zl3, h0r, *lru_args(1), rev=True, hf=hlf)

        xl = _out_proj(attn, mlp, lru_l.reshape(batch * seq, LRU_WIDTH), xl, mod, g[1], w_out_b, l, seq, 512)
        xl = _ffn(xl, mod, g[2], g[3], w_fi_b, w_fo_b, l, seq, 512, 512)

        if not last:
            attn_c = _attention(zc, zc, g_qk[l], cos, sin_signed, batch, ctx_len, ctx_len, False, ctx_len)
            mlp_c = _chunk_mlp(zc, ws, bias, 512)
            xc = _out_proj(attn_c, mlp_c, lru_c.reshape(batch * ctx_len, LRU_WIDTH), xc, mod, g[1], w_out_b,
                           l, None, 512)
            xc = _ffn(xc, mod, g[2], g[3], w_fi_b, w_fo_b, l, None, 512, 512)

    return xl.reshape(batch, seq, d)
```

```python
import functools

import jax
import jax.numpy as jnp
from jax import lax
from jax.experimental import pallas as pl
from jax.experimental.pallas import tpu as pltpu

F32 = jnp.float32
BF16 = jnp.bfloat16

D_MODEL = 2048
GRID_W = 64
EPS = 1e-6
HEAD_DIM = 128
N_Q_HEADS = 8
N_KV_HEADS = 2
Q_PER_KV = 4
ATTN_WIDTH = 1024
KV_WIDTH = 256
ROPE_THETA = 10000.0
AXIS_DIM = 64
CHUNK = 128
MLP_WIDTH = 512
MLP_GROUPS = 4
LRU_WIDTH = 512
LRU_BLOCKS = 4
LRU_C = 8.0
IN_COLS = 3584
D_FF = 5632

K_COL128 = ATTN_WIDTH // 128
V_COL128 = (ATTN_WIDTH + KV_WIDTH) // 128
ZM_COL512 = (ATTN_WIDTH + 2 * KV_WIDTH) // 512
ZX_COL512 = ZM_COL512 + 2
ZG_COL512 = ZX_COL512 + 1

VMEM_LIMIT = 56 * 1024 * 1024
MOD_ROWS = 16
ATTN_KC = 256
LOG2_E = 1.4426950408889634
FFN_TF = 512
LRU_T = 256
LRU_PAD = 8


def _params(sem):
    return pltpu.CompilerParams(dimension_semantics=sem, vmem_limit_bytes=VMEM_LIMIT)


def _unit_rms(xf):
    return xf * lax.rsqrt(jnp.mean(xf * xf, axis=-1, keepdims=True) + EPS)


def _norm_modulate(xf, g, shift, scale):
    return _unit_rms(xf) * (g * (1.0 + scale)) + shift


def _gated_norm_residual(x, y, g, gate):
    return x + _unit_rms(y) * (g * gate)


def _mod_kernel(c_ref, w_ref, b_ref, o_ref):
    c = c_ref[...]
    a = (c * jax.nn.sigmoid(c)).astype(BF16)
    o_ref[0] = jnp.dot(a, w_ref[0].astype(BF16), preferred_element_type=F32) + b_ref[0]


def _modulation(c_all, w_mod, b_mod):
    depth, d, n = w_mod.shape
    tn = 1024
    return pl.pallas_call(
        _mod_kernel,
        grid=(depth, n // tn),
        in_specs=[
            pl.BlockSpec((MOD_ROWS, d), lambda l, j: (0, 0)),
            pl.BlockSpec((1, d, tn), lambda l, j: (l, 0, j)),
            pl.BlockSpec((1, 1, tn), lambda l, j: (l, 0, j)),
        ],
        out_specs=pl.BlockSpec((1, MOD_ROWS, tn), lambda l, j: (l, 0, j)),
        out_shape=jax.ShapeDtypeStruct((depth, MOD_ROWS, n), F32),
        compiler_params=_params(("arbitrary", "arbitrary")),
        name="modulation",
    )(c_all, w_mod, b_mod.reshape(depth, 1, n))


def _in_proj_kernel(x_ref, mod_ref, g_ref, w_ref, z_ref):
    m = mod_ref[0]
    h = _norm_modulate(x_ref[...], g_ref[...], m[0:1], m[1:2])
    z_ref[...] = jnp.dot(h.astype(BF16), w_ref[...], preferred_element_type=F32).astype(BF16)


def _in_proj(x, mod, g, w, layer, rows_per_mod, tm):
    M = x.shape[0]
    if rows_per_mod is None:
        mod_map = lambda i: (8, 0, 0)
    else:
        blocks = rows_per_mod // tm
        mod_map = lambda i: (i // blocks, 0, 0)
    return pl.pallas_call(
        _in_proj_kernel,
        grid=(M // tm,),
        in_specs=[
            pl.BlockSpec((tm, D_MODEL), lambda i: (i, 0)),
            pl.BlockSpec((1, 6, D_MODEL), mod_map),
            pl.BlockSpec((1, D_MODEL), lambda i: (0, 0)),
            pl.BlockSpec((None, D_MODEL, IN_COLS), lambda i: (layer, 0, 0)),
        ],
        out_specs=pl.BlockSpec((tm, IN_COLS), lambda i: (i, 0)),
        out_shape=jax.ShapeDtypeStruct((M, IN_COLS), BF16),
        compiler_params=_params(("arbitrary",)),
        name="in_proj",
    )(x, mod, g, w)


def _head_norm(xf, g):
    return xf * lax.rsqrt(jnp.mean(xf * xf, axis=-1, keepdims=True) + EPS) * g


def _rope(xf, cos, sin_signed):
    lane = lax.broadcasted_iota(jnp.int32, xf.shape, 1)
    first = (lane % AXIS_DIM) < (AXIS_DIM // 2)
    partner = jnp.where(first, pltpu.roll(xf, HEAD_DIM - AXIS_DIM // 2, 1), pltpu.roll(xf, AXIS_DIM // 2, 1))
    return xf * cos + partner * sin_signed


_NT = (((1,), (1,)), ((), ()))


def _attn_kernel(*refs, tq, ctx_len, has_lat):
    if has_lat:
        (q_ref, kc_ref, vc_ref, kl_ref, vl_ref, cos_ref, sin_ref, gqk_ref, o_ref, kn_scr, vt_scr) = refs
    else:
        (q_ref, kc_ref, vc_ref, gqk_ref, o_ref, kn_scr, vt_scr) = refs
    qi = pl.program_id(2)
    gq = gqk_ref[0:1, :]
    gk = gqk_ref[1:2, :]
    n_keys = kn_scr.shape[0]

    @pl.when(qi == 0)
    def _():
        kn_scr[0:ctx_len, :] = _head_norm(kc_ref[...].astype(F32), gk).astype(BF16)
        vt_scr[:, 0:ctx_len] = vc_ref[...].astype(F32).T.astype(BF16)
        if has_lat:
            k = _head_norm(kl_ref[...].astype(F32), gk)
            kn_scr[ctx_len:, :] = _rope(k, cos_ref[...], sin_ref[...]).astype(BF16)
            vt_scr[:, ctx_len:] = vl_ref[...].astype(F32).T.astype(BF16)

    scale = HEAD_DIM ** -0.5 * LOG2_E
    heads = []
    for h in range(Q_PER_KV):
        qh = _head_norm(q_ref[:, h * HEAD_DIM:(h + 1) * HEAD_DIM].astype(F32), gq)
        if has_lat:
            rows = pl.ds(pl.multiple_of(qi * tq, tq), tq)
            qh = _rope(qh, cos_ref[rows, :], sin_ref[rows, :])
        heads.append((qh * scale).astype(BF16))

    qs = jnp.concatenate(heads, axis=0)

    s = lax.dot_general(kn_scr[...], qs, _NT, preferred_element_type=F32)
    m = jnp.max(s, axis=0, keepdims=True)
    l = jnp.zeros((1, Q_PER_KV * tq), F32)
    acc = jnp.zeros((HEAD_DIM, Q_PER_KV * tq), F32)
    for c in range(n_keys // ATTN_KC):
        rows = slice(c * ATTN_KC, (c + 1) * ATTN_KC)
        p = jnp.exp2(s[rows] - m)
        l = l + jnp.sum(p, axis=0, keepdims=True)
        acc = acc + jnp.dot(vt_scr[:, rows], p.astype(BF16), preferred_element_type=F32)
    o = (acc / l).T
    for h in range(Q_PER_KV):
        o_ref[:, h * HEAD_DIM:(h + 1) * HEAD_DIM] = o[h * tq:(h + 1) * tq].astype(BF16)


def _attention(zq, zc, gqk, cos, sin_signed, batch, q_len, ctx_len, has_lat, tq):
    nq = q_len // tq
    n_keys = ctx_len + (q_len if has_lat else 0)
    kv_c = lambda col: pl.BlockSpec((ctx_len, HEAD_DIM), lambda b, k, i: (b, col + k))
    kv_l = lambda col: pl.BlockSpec((q_len, HEAD_DIM), lambda b, k, i: (b, col + k))
    in_specs = [pl.BlockSpec((tq, Q_PER_KV * HEAD_DIM), lambda b, k, i: (b * nq + i, k)),
                kv_c(K_COL128), kv_c(V_COL128)]
    args = [zq, zc, zc]
    scratch = [pltpu.VMEM((n_keys, HEAD_DIM), BF16), pltpu.VMEM((HEAD_DIM, n_keys), BF16)]
    if has_lat:
        in_specs += [kv_l(K_COL128), kv_l(V_COL128),
                     pl.BlockSpec((q_len, HEAD_DIM), lambda b, k, i: (0, 0)),
                     pl.BlockSpec((q_len, HEAD_DIM), lambda b, k, i: (0, 0))]
        args += [zq, zq, cos, sin_signed]
    in_specs.append(pl.BlockSpec((2, HEAD_DIM), lambda b, k, i: (0, 0)))
    args.append(gqk)
    return pl.pallas_call(
        functools.partial(_attn_kernel, tq=tq, ctx_len=ctx_len, has_lat=has_lat),
        grid=(batch, N_KV_HEADS, nq),
        in_specs=in_specs,
        out_specs=pl.BlockSpec((tq, Q_PER_KV * HEAD_DIM), lambda b, k, i: (b * nq + i, k)),
        out_shape=jax.ShapeDtypeStruct((batch * q_len, ATTN_WIDTH), BF16),
        scratch_shapes=scratch,
        compiler_params=_params(("arbitrary", "arbitrary", "arbitrary")),
        name="attention_lat" if has_lat else "attention_ctx",
    )(*args)


def _cmlp_kernel(zu_ref, zv_ref, ws_ref, bs_ref, o_ref, *, tm):
    bias = bs_ref[...]
    for n in range(tm // CHUNK):
        rows = slice(n * CHUNK, (n + 1) * CHUNK)
        v = jax.nn.gelu(zv_ref[rows, :].astype(F32)).astype(BF16)
        parts = [jnp.dot(ws_ref[g], v[:, g * 128:(g + 1) * 128], preferred_element_type=F32)
                 for g in range(MLP_GROUPS)]
        s = jnp.concatenate(parts, axis=-1) + bias
        u = jax.nn.gelu(zu_ref[rows, :].astype(F32))
        o_ref[rows, :] = (u * s).astype(BF16)


def _chunk_mlp(z, ws, bias, tm):
    M = z.shape[0]
    return pl.pallas_call(
        functools.partial(_cmlp_kernel, tm=tm),
        grid=(M // tm,),
        in_specs=[
            pl.BlockSpec((tm, MLP_WIDTH), lambda i: (i, ZM_COL512)),
            pl.BlockSpec((tm, MLP_WIDTH), lambda i: (i, ZM_COL512 + 1)),
            pl.BlockSpec((MLP_GROUPS, CHUNK, CHUNK), lambda i: (0, 0, 0)),
            pl.BlockSpec((CHUNK, MLP_WIDTH), lambda i: (0, 0)),
        ],
        out_specs=pl.BlockSpec((tm, MLP_WIDTH), lambda i: (i, 0)),
        out_shape=jax.ShapeDtypeStruct((M, MLP_WIDTH), BF16),
        compiler_params=_params(("arbitrary",)),
        name="chunk_mlp",
    )(z, z, ws, bias)


def _lru_kernel(*refs, T, nchunks, rev, batch):
    if rev:
        (zx_ref, zp_ref, zn_ref, h0_ref, cw_ref, cb_ref, w_ref, b_ref, lam_ref, zg_ref, hf_ref,
         out_ref, hfin_ref, a_scr, bx_scr, hc_scr) = refs
    else:
        (zx_ref, zp_ref, zn_ref, h0_ref, cw_ref, cb_ref, w_ref, b_ref, lam_ref,
         out_ref, hfin_ref, a_scr, bx_scr, hc_scr) = refs
    S = T + LRU_PAD
    s = pl.program_id(0)
    c = (nchunks - 1 - s) if rev else s

    @pl.when(s == 0)
    def _():
        hc_scr[...] = h0_ref[...]

    cw = cw_ref[...]
    cb = cb_ref[...]
    lam = lam_ref[...]
    neg_lam = -lam
    softplus = jnp.maximum(neg_lam, 0.0) + jnp.log1p(jnp.exp(-jnp.abs(neg_lam)))
    half_coef2 = (-0.5 * LRU_C * LOG2_E) * softplus
    half_ba = 0.5 * b_ref[0:1, :]
    half_bi = 0.5 * b_ref[1:2, :]
    keep_prev = jnp.where(c == 0, 0.0, 1.0)
    keep_next = jnp.where(c == nchunks - 1, 0.0, 1.0)
    TE = T + 16

    for b in range(batch):
        x = zx_ref[b].astype(F32)
        xe = jnp.concatenate([zp_ref[b].astype(F32) * keep_prev, x, zn_ref[b].astype(F32) * keep_next], axis=0)
        xm2 = pltpu.roll(xe, 2, 0)[8:T + 8]
        xm1 = pltpu.roll(xe, 1, 0)[8:T + 8]
        xp1 = pltpu.roll(xe, TE - 1, 0)[8:T + 8]
        xr = cb + xm2 * cw[0:1] + xm1 * cw[1:2] + x * cw[2:3] + xp1 * cw[3:4]
        xb = xr.astype(BF16)
        ga = jnp.concatenate(
            [jnp.dot(xb[:, k * 128:(k + 1) * 128], w_ref[0, k], preferred_element_type=F32)
             for k in range(LRU_BLOCKS)], axis=-1)
        gi = jnp.concatenate(
            [jnp.dot(xb[:, k * 128:(k + 1) * 128], w_ref[1, k], preferred_element_type=F32)
             for k in range(LRU_BLOCKS)], axis=-1)
        a = jnp.exp2(half_coef2 * (1.0 + jnp.tanh(ga + half_ba)))
        i = 0.5 * (1.0 + jnp.tanh(gi + half_bi))
        y = 1.0 - a * a
        bx = jnp.where(y > 0.0, y * lax.rsqrt(y), 0.0) * (i * xr)
        for k in range(LRU_BLOCKS):
            a_scr[k, b * S:b * S + T, :] = a[:, k * 128:(k + 1) * 128]
            bx_scr[k, b * S:b * S + T, :] = bx[:, k * 128:(k + 1) * 128]

    def step(n, hs):
        tt = (T - 1 - n) if rev else n
        rows = pl.ds(tt, batch, stride=S)
        out = []
        for k in range(LRU_BLOCKS):
            h = a_scr[k, rows, :] * hs[k] + bx_scr[k, rows, :]
            bx_scr[k, rows, :] = h
            out.append(h)
        return tuple(out)

    h_init = tuple(hc_scr[:, k * 128:(k + 1) * 128] for k in range(LRU_BLOCKS))
    h_last = jnp.concatenate(lax.fori_loop(0, T, step, h_init, unroll=4), axis=-1)
    hc_scr[...] = h_last
    hfin_ref[...] = h_last

    for b in range(batch):
        hb = jnp.concatenate([bx_scr[k, b * S:b * S + T, :] for k in range(LRU_BLOCKS)], axis=-1)
        if rev:
            gate = jax.nn.gelu(zg_ref[b].astype(F32))
            out_ref[b] = ((hf_ref[b] + hb) * gate).astype(BF16)
        else:
            out_ref[b] = hb


def _lru_pass(z3, h0, cw, cb, w, bvec, lam, rev, hf=None):
    batch, L, _ = z3.shape
    T = min(LRU_T, L)
    nchunks = L // T
    nb8 = L // 8
    cidx = (lambda s: nchunks - 1 - s) if rev else (lambda s: s)
    main = lambda col: pl.BlockSpec((batch, T, LRU_WIDTH), lambda s: (0, cidx(s), col))
    prev_map = lambda s: (0, jnp.maximum(cidx(s) * (T // 8) - 1, 0), ZX_COL512)
    next_map = lambda s: (0, jnp.minimum((cidx(s) + 1) * (T // 8), nb8 - 1), ZX_COL512)
    const2 = lambda shape: pl.BlockSpec(shape, lambda s: (0, 0))
    in_specs = [main(ZX_COL512),
                pl.BlockSpec((batch, 8, LRU_WIDTH), prev_map),
                pl.BlockSpec((batch, 8, LRU_WIDTH), next_map),
                const2((batch, LRU_WIDTH)),
                const2((4, LRU_WIDTH)), const2((1, LRU_WIDTH)),
                pl.BlockSpec((2, LRU_BLOCKS, 128, 128), lambda s: (0, 0, 0, 0)),
                const2((2, LRU_WIDTH)), const2((1, LRU_WIDTH))]
    args = [z3, z3, z3, h0, cw, cb, w, bvec, lam]
    if rev:
        in_specs += [main(ZG_COL512), pl.BlockSpec((batch, T, LRU_WIDTH), lambda s: (0, cidx(s), 0))]
        args += [z3, hf]
    out_dtype = BF16 if rev else F32
    S = T + LRU_PAD
    return pl.pallas_call(
        functools.partial(_lru_kernel, T=T, nchunks=nchunks, rev=rev, batch=batch),
        grid=(nchunks,),
        in_specs=in_specs,
        out_specs=[pl.BlockSpec((batch, T, LRU_WIDTH), lambda s: (0, cidx(s), 0)),
                   const2((batch, LRU_WIDTH))],
        out_shape=[jax.ShapeDtypeStruct((batch, L, LRU_WIDTH), out_dtype),
                   jax.ShapeDtypeStruct((batch, LRU_WIDTH), F32)],
        scratch_shapes=[pltpu.VMEM((LRU_BLOCKS, batch * S, 128), F32),
                        pltpu.VMEM((LRU_BLOCKS, batch * S, 128), F32),
                        pltpu.VMEM((batch, LRU_WIDTH), F32)],
        compiler_params=_params(("arbitrary",)),
        name="lru_rev" if rev else "lru_fwd",
    )(*args)


def _out_proj_kernel(a_ref, m_ref, r_ref, x_ref, mod_ref, g_ref, w_ref, o_ref):
    mix = jnp.concatenate([a_ref[...], m_ref[...], r_ref[...]], axis=-1)
    y = jnp.dot(mix, w_ref[...], preferred_element_type=F32)
    o_ref[...] = _gated_norm_residual(x_ref[...], y, g_ref[...], mod_ref[0][2:3])


def _out_proj(attn, mlp, lru, x, mod, g, w, layer, rows_per_mod, tm):
    M = x.shape[0]
    if rows_per_mod is None:
        mod_map = lambda i: (8, 0, 0)
    else:
        blocks = rows_per_mod // tm
        mod_map = lambda i: (i // blocks, 0, 0)
    return pl.pallas_call(
        _out_proj_kernel,
        grid=(M // tm,),
        in_specs=[
            pl.BlockSpec((tm, ATTN_WIDTH), lambda i: (i, 0)),
            pl.BlockSpec((tm, MLP_WIDTH), lambda i: (i, 0)),
            pl.BlockSpec((tm, LRU_WIDTH), lambda i: (i, 0)),
            pl.BlockSpec((tm, D_MODEL), lambda i: (i, 0)),
            pl.BlockSpec((1, 6, D_MODEL), mod_map),
            pl.BlockSpec((1, D_MODEL), lambda i: (0, 0)),
            pl.BlockSpec((None, D_MODEL, D_MODEL), lambda i: (layer, 0, 0)),
        ],
        out_specs=pl.BlockSpec((tm, D_MODEL), lambda i: (i, 0)),
        out_shape=jax.ShapeDtypeStruct((M, D_MODEL), F32),
        compiler_params=_params(("arbitrary",)),
        name="out_proj",
    )(attn, mlp, lru, x, mod, g, w)


def _ffn_kernel(x_ref, mod_ref, g2_ref, g3_ref, wgu_ref, wo_ref, o_ref, h_scr, acc_scr):
    j = pl.program_id(1)
    m = mod_ref[0]
    tf = wo_ref.shape[0]

    @pl.when(j == 0)
    def _():
        h = _norm_modulate(x_ref[...], g2_ref[...], m[3:4], m[4:5])
        h_scr[...] = h.astype(BF16)
        acc_scr[...] = jnp.zeros_like(acc_scr)

    h = h_scr[...]
    gu = jnp.dot(h, wgu_ref[...], preferred_element_type=F32)
    gate = gu[:, :tf]
    up = gu[:, tf:]
    act = (gate * jax.nn.sigmoid(gate) * up).astype(BF16)
    acc_scr[...] += jnp.dot(act, wo_ref[...], preferred_element_type=F32)

    @pl.when(j == pl.num_programs(1) - 1)
    def _():
        o_ref[...] = _gated_norm_residual(x_ref[...], acc_scr[...], g3_ref[...], m[5:6])


def _ffn(x, mod, g2, g3, w_in, w_out, layer, rows_per_mod, tm, tf):
    M = x.shape[0]
    nf = D_FF // tf
    if rows_per_mod is None:
        mod_map = lambda i, j: (8, 0, 0)
    else:
        blocks = rows_per_mod // tm
        mod_map = lambda i, j: (i // blocks, 0, 0)
    return pl.pallas_call(
        _ffn_kernel,
        grid=(M // tm, nf),
        in_specs=[
            pl.BlockSpec((tm, D_MODEL), lambda i, j: (i, 0)),
            pl.BlockSpec((1, 6, D_MODEL), mod_map),
            pl.BlockSpec((1, D_MODEL), lambda i, j: (0, 0)),
            pl.BlockSpec((1, D_MODEL), lambda i, j: (0, 0)),
            pl.BlockSpec((None, D_MODEL, 2 * tf), lambda i, j: (layer, 0, j)),
            pl.BlockSpec((None, tf, D_MODEL), lambda i, j: (layer, j, 0)),
        ],
        out_specs=pl.BlockSpec((tm, D_MODEL), lambda i, j: (i, 0)),
        out_shape=jax.ShapeDtypeStruct((M, D_MODEL), F32),
        scratch_shapes=[pltpu.VMEM((tm, D_MODEL), BF16), pltpu.VMEM((tm, D_MODEL), F32)],
        compiler_params=_params(("arbitrary", "arbitrary")),
        name="ffn",
    )(x, mod, g2, g3, w_in, w_out)


def _rope_tables(n_tokens):
    rows = n_tokens // GRID_W
    row_ids = jnp.repeat(jnp.arange(rows, dtype=F32), GRID_W)
    col_ids = jnp.tile(jnp.arange(GRID_W, dtype=F32), rows)
    inv_freq = ROPE_THETA ** (-jnp.arange(0, AXIS_DIM, 2, dtype=F32) / AXIS_DIM)
    ang_r = row_ids[:, None] * inv_freq
    ang_c = col_ids[:, None] * inv_freq
    ang = jnp.concatenate([ang_r, ang_r, ang_c, ang_c], axis=-1)
    sign = jnp.where((jnp.arange(HEAD_DIM) % AXIS_DIM) < (AXIS_DIM // 2), -1.0, 1.0).astype(F32)
    return jnp.cos(ang), jnp.sin(ang) * sign


def kernel(x, c, ctx, c_ctx, w_mod, b_mod, g_norm, w_in, g_qk, w_s, b_s, conv_w, conv_b,
           lru_w, lru_b, lru_lam, w_out, w_ffn_in, w_ffn_out):
    batch, seq, d = x.shape
    ctx_len = ctx.shape[1]
    depth = w_mod.shape[0]
    cos, sin_signed = _rope_tables(seq)

    c_all = jnp.concatenate([c, c_ctx[None, :], jnp.zeros((MOD_ROWS - batch - 1, d), F32)], axis=0)
    mod_all = _modulation(c_all, w_mod, b_mod).reshape(depth, MOD_ROWS, 6, d)

    xl = x.reshape(batch * seq, d)
    xc = ctx.reshape(batch * ctx_len, d)
    zeros_h = jnp.zeros((batch, LRU_WIDTH), F32)
    w_in_b = w_in.astype(BF16)
    w_out_b = w_out.astype(BF16)
    w_fi_b = (w_ffn_in.astype(BF16).reshape(depth, d, 2, D_FF // FFN_TF, FFN_TF)
              .transpose(0, 1, 3, 2, 4).reshape(depth, d, 2 * D_FF))
    w_fo_b = w_ffn_out.astype(BF16)

    for l in range(depth):
        last = l == depth - 1
        mod = mod_all[l]
        g = g_norm[l].reshape(4, 1, d)
        ws = w_s[l].astype(BF16)
        bias = jnp.repeat(b_s[l].T, CHUNK, axis=1)
        lw = (0.5 * lru_w[l]).astype(BF16)
        cb = conv_b[l].reshape(1, LRU_WIDTH)

        zl = _in_proj(xl, mod, g[0], w_in_b, l, seq, 512)
        zc = _in_proj(xc, mod, g[0], w_in_b, l, None, 512)

        attn = _attention(zl, zc, g_qk[l], cos, sin_signed, batch, seq, ctx_len, True, 512)
        mlp = _chunk_mlp(zl, ws, bias, 512)

        zc3 = zc.reshape(batch, ctx_len, IN_COLS)
        zl3 = zl.reshape(batch, seq, IN_COLS)
        lru_args = lambda dd: (conv_w[l], cb, lw[dd], lru_b[l, dd], lru_lam[l, dd].reshape(1, LRU_WIDTH))
        hcf, h0f = _lru_pass(zc3, zeros_h, *lru_args(0), rev=False)
        lru_c, h0r = _lru_pass(zc3, zeros_h, *lru_args(1), rev=True, hf=hcf)
        hlf, _ = _lru_pass(zl3, h0f, *lru_args(0), rev=False)
        lru_l, _ = _lru_pass(zl3, h0r, *lru_args(1), rev=True, hf=hlf)

        xl = _out_proj(attn, mlp, lru_l.reshape(batch * seq, LRU_WIDTH), xl, mod, g[1], w_out_b, l, seq, 512)
        xl = _ffn(xl, mod, g[2], g[3], w_fi_b, w_fo_b, l, seq, 512, FFN_TF)

        if not last:
            attn_c = _attention(zc, zc, g_qk[l], cos, sin_signed, batch, ctx_len, ctx_len, False, ctx_len)
            mlp_c = _chunk_mlp(zc, ws, bias, 512)
            xc = _out_proj(attn_c, mlp_c, lru_c.reshape(batch * ctx_len, LRU_WIDTH), xc, mod, g[1], w_out_b,
                           l, None, 512)
            xc = _ffn(xc, mod, g[2], g[3], w_fi_b, w_fo_b, l, None, 512, FFN_TF)

    return xl.reshape(batch, seq, d)
```

```python
import functools

import jax
import jax.numpy as jnp
from jax import lax
from jax.experimental import pallas as pl
from jax.experimental.pallas import tpu as pltpu

F32 = jnp.float32
BF16 = jnp.bfloat16

D_MODEL = 2048
GRID_W = 64
EPS = 1e-6
HEAD_DIM = 128
N_Q_HEADS = 8
N_KV_HEADS = 2
Q_PER_KV = 4
ATTN_WIDTH = 1024
KV_WIDTH = 256
ROPE_THETA = 10000.0
AXIS_DIM = 64
CHUNK = 128
MLP_WIDTH = 512
MLP_GROUPS = 4
LRU_WIDTH = 512
LRU_BLOCKS = 4
LRU_C = 8.0
IN_COLS = 3584
D_FF = 5632

K_COL128 = ATTN_WIDTH // 128
V_COL128 = (ATTN_WIDTH + KV_WIDTH) // 128
ZM_COL512 = (ATTN_WIDTH + 2 * KV_WIDTH) // 512
ZX_COL512 = ZM_COL512 + 2
ZG_COL512 = ZX_COL512 + 1

VMEM_LIMIT = 56 * 1024 * 1024
MOD_ROWS = 16
ATTN_KC = 256
LOG2_E = 1.4426950408889634
FFN_TF = 512
LRU_T = 256
LRU_PAD = 8


def _params(sem):
    return pltpu.CompilerParams(dimension_semantics=sem, vmem_limit_bytes=VMEM_LIMIT)


def _unit_rms(xf):
    return xf * lax.rsqrt(jnp.mean(xf * xf, axis=-1, keepdims=True) + EPS)


def _norm_modulate(xf, g, shift, scale):
    return _unit_rms(xf) * (g * (1.0 + scale)) + shift


def _gated_norm_residual(x, y, g, gate):
    return x + _unit_rms(y) * (g * gate)


def _mod_kernel(c_ref, w_ref, b_ref, o_ref):
    c = c_ref[...]
    a = (c * jax.nn.sigmoid(c)).astype(BF16)
    o_ref[0] = jnp.dot(a, w_ref[0].astype(BF16), preferred_element_type=F32) + b_ref[0]


def _modulation(c_all, w_mod, b_mod):
    depth, d, n = w_mod.shape
    tn = 1024
    return pl.pallas_call(
        _mod_kernel,
        grid=(depth, n // tn),
        in_specs=[
            pl.BlockSpec((MOD_ROWS, d), lambda l, j: (0, 0)),
            pl.BlockSpec((1, d, tn), lambda l, j: (l, 0, j)),
            pl.BlockSpec((1, 1, tn), lambda l, j: (l, 0, j)),
        ],
        out_specs=pl.BlockSpec((1, MOD_ROWS, tn), lambda l, j: (l, 0, j)),
        out_shape=jax.ShapeDtypeStruct((depth, MOD_ROWS, n), F32),
        compiler_params=_params(("arbitrary", "arbitrary")),
        name="modulation",
    )(c_all, w_mod, b_mod.reshape(depth, 1, n))


def _in_proj_kernel(x_ref, mod_ref, g_ref, w_ref, z_ref):
    m = mod_ref[0]
    h = _norm_modulate(x_ref[...], g_ref[...], m[0:1], m[1:2])
    z_ref[...] = jnp.dot(h.astype(BF16), w_ref[...], preferred_element_type=F32).astype(BF16)


def _in_proj(x, mod, g, w, layer, rows_per_mod, tm):
    M = x.shape[0]
    if rows_per_mod is None:
        mod_map = lambda i: (8, 0, 0)
    else:
        blocks = rows_per_mod // tm
        mod_map = lambda i: (i // blocks, 0, 0)
    return pl.pallas_call(
        _in_proj_kernel,
        grid=(M // tm,),
        in_specs=[
            pl.BlockSpec((tm, D_MODEL), lambda i: (i, 0)),
            pl.BlockSpec((1, 6, D_MODEL), mod_map),
            pl.BlockSpec((1, D_MODEL), lambda i: (0, 0)),
            pl.BlockSpec((None, D_MODEL, IN_COLS), lambda i: (layer, 0, 0)),
        ],
        out_specs=pl.BlockSpec((tm, IN_COLS), lambda i: (i, 0)),
        out_shape=jax.ShapeDtypeStruct((M, IN_COLS), BF16),
        compiler_params=_params(("arbitrary",)),
        name="in_proj",
    )(x, mod, g, w)


def _head_norm(xf, g):
    return xf * lax.rsqrt(jnp.mean(xf * xf, axis=-1, keepdims=True) + EPS) * g


def _rope(xf, cos, sin_signed):
    lane = lax.broadcasted_iota(jnp.int32, xf.shape, 1)
    first = (lane % AXIS_DIM) < (AXIS_DIM // 2)
    partner = jnp.where(first, pltpu.roll(xf, HEAD_DIM - AXIS_DIM // 2, 1), pltpu.roll(xf, AXIS_DIM // 2, 1))
    return xf * cos + partner * sin_signed


_NT = (((1,), (1,)), ((), ()))


def _attn_kernel(*refs, tq, ctx_len, has_lat):
    if has_lat:
        (q_ref, kc_ref, vc_ref, kl_ref, vl_ref, cos_ref, sin_ref, gqk_ref, o_ref, kn_scr, vt_scr) = refs
    else:
        (q_ref, kc_ref, vc_ref, gqk_ref, o_ref, kn_scr, vt_scr) = refs
    qi = pl.program_id(2)
    gq = gqk_ref[0:1, :]
    gk = gqk_ref[1:2, :]
    n_keys = kn_scr.shape[0]

    @pl.when(qi == 0)
    def _():
        kn_scr[0:ctx_len, :] = _head_norm(kc_ref[...].astype(F32), gk).astype(BF16)
        vt_scr[:, 0:ctx_len] = vc_ref[...].astype(F32).T.astype(BF16)
        if has_lat:
            k = _head_norm(kl_ref[...].astype(F32), gk)
            kn_scr[ctx_len:, :] = _rope(k, cos_ref[...], sin_ref[...]).astype(BF16)
            vt_scr[:, ctx_len:] = vl_ref[...].astype(F32).T.astype(BF16)

    scale = HEAD_DIM ** -0.5 * LOG2_E
    heads = []
    for h in range(Q_PER_KV):
        qh = _head_norm(q_ref[:, h * HEAD_DIM:(h + 1) * HEAD_DIM].astype(F32), gq)
        if has_lat:
            rows = pl.ds(pl.multiple_of(qi * tq, tq), tq)
            qh = _rope(qh, cos_ref[rows, :], sin_ref[rows, :])
        heads.append((qh * scale).astype(BF16))

    qs = jnp.concatenate(heads, axis=0)

    s = lax.dot_general(kn_scr[...], qs, _NT, preferred_element_type=F32)
    m = jnp.max(s, axis=0, keepdims=True)
    l = jnp.zeros((1, Q_PER_KV * tq), F32)
    acc = jnp.zeros((HEAD_DIM, Q_PER_KV * tq), F32)
    for c in range(n_keys // ATTN_KC):
        rows = slice(c * ATTN_KC, (c + 1) * ATTN_KC)
        p = jnp.exp2(s[rows] - m)
        l = l + jnp.sum(p, axis=0, keepdims=True)
        acc = acc + jnp.dot(vt_scr[:, rows], p.astype(BF16), preferred_element_type=F32)
    o = (acc / l).T
    for h in range(Q_PER_KV):
        o_ref[:, h * HEAD_DIM:(h + 1) * HEAD_DIM] = o[h * tq:(h + 1) * tq].astype(BF16)


def _attention(zq, zc, gqk, cos, sin_signed, batch, q_len, ctx_len, has_lat, tq):
    nq = q_len // tq
    n_keys = ctx_len + (q_len if has_lat else 0)
    kv_c = lambda col: pl.BlockSpec((ctx_len, HEAD_DIM), lambda b, k, i: (b, col + k))
    kv_l = lambda col: pl.BlockSpec((q_len, HEAD_DIM), lambda b, k, i: (b, col + k))
    in_specs = [pl.BlockSpec((tq, Q_PER_KV * HEAD_DIM), lambda b, k, i: (b * nq + i, k)),
                kv_c(K_COL128), kv_c(V_COL128)]
    args = [zq, zc, zc]
    scratch = [pltpu.VMEM((n_keys, HEAD_DIM), BF16), pltpu.VMEM((HEAD_DIM, n_keys), BF16)]
    if has_lat:
        in_specs += [kv_l(K_COL128), kv_l(V_COL128),
                     pl.BlockSpec((q_len, HEAD_DIM), lambda b, k, i: (0, 0)),
                     pl.BlockSpec((q_len, HEAD_DIM), lambda b, k, i: (0, 0))]
        args += [zq, zq, cos, sin_signed]
    in_specs.append(pl.BlockSpec((2, HEAD_DIM), lambda b, k, i: (0, 0)))
    args.append(gqk)
    return pl.pallas_call(
        functools.partial(_attn_kernel, tq=tq, ctx_len=ctx_len, has_lat=has_lat),
        grid=(batch, N_KV_HEADS, nq),
        in_specs=in_specs,
        out_specs=pl.BlockSpec((tq, Q_PER_KV * HEAD_DIM), lambda b, k, i: (b * nq + i, k)),
        out_shape=jax.ShapeDtypeStruct((batch * q_len, ATTN_WIDTH), BF16),
        scratch_shapes=scratch,
        compiler_params=_params(("arbitrary", "arbitrary", "arbitrary")),
        name="attention_lat" if has_lat else "attention_ctx",
    )(*args)


def _cmlp_kernel(zu_ref, zv_ref, ws_ref, bs_ref, o_ref, *, tm):
    bias = bs_ref[...]
    for n in range(tm // CHUNK):
        rows = slice(n * CHUNK, (n + 1) * CHUNK)
        v = jax.nn.gelu(zv_ref[rows, :].astype(F32)).astype(BF16)
        parts = [jnp.dot(ws_ref[g], v[:, g * 128:(g + 1) * 128], preferred_element_type=F32)
                 for g in range(MLP_GROUPS)]
        s = jnp.concatenate(parts, axis=-1) + bias
        u = jax.nn.gelu(zu_ref[rows, :].astype(F32))
        o_ref[rows, :] = (u * s).astype(BF16)


def _chunk_mlp(z, ws, bias, tm):
    M = z.shape[0]
    return pl.pallas_call(
        functools.partial(_cmlp_kernel, tm=tm),
        grid=(M // tm,),
        in_specs=[
            pl.BlockSpec((tm, MLP_WIDTH), lambda i: (i, ZM_COL512)),
            pl.BlockSpec((tm, MLP_WIDTH), lambda i: (i, ZM_COL512 + 1)),
            pl.BlockSpec((MLP_GROUPS, CHUNK, CHUNK), lambda i: (0, 0, 0)),
            pl.BlockSpec((CHUNK, MLP_WIDTH), lambda i: (0, 0)),
        ],
        out_specs=pl.BlockSpec((tm, MLP_WIDTH), lambda i: (i, 0)),
        out_shape=jax.ShapeDtypeStruct((M, MLP_WIDTH), BF16),
        compiler_params=_params(("arbitrary",)),
        name="chunk_mlp",
    )(z, z, ws, bias)


def _lru_kernel(*refs, T, nchunks, rev, batch):
    if rev:
        (zx_ref, zp_ref, zn_ref, h0_ref, cw_ref, cb_ref, w_ref, b_ref, lam_ref, zg_ref, hf_ref,
         out_ref, hfin_ref, a_scr, bx_scr, hc_scr) = refs
    else:
        (zx_ref, zp_ref, zn_ref, h0_ref, cw_ref, cb_ref, w_ref, b_ref, lam_ref,
         out_ref, hfin_ref, a_scr, bx_scr, hc_scr) = refs
    S = T + LRU_PAD
    s = pl.program_id(0)
    c = (nchunks - 1 - s) if rev else s

    @pl.when(s == 0)
    def _():
        hc_scr[...] = h0_ref[...]

    cw = cw_ref[...]
    cb = cb_ref[...]
    lam = lam_ref[...]
    neg_lam = -lam
    softplus = jnp.maximum(neg_lam, 0.0) + jnp.log1p(jnp.exp(-jnp.abs(neg_lam)))
    half_coef2 = (-0.5 * LRU_C * LOG2_E) * softplus
    half_ba = 0.5 * b_ref[0:1, :]
    half_bi = 0.5 * b_ref[1:2, :]
    keep_prev = jnp.where(c == 0, 0.0, 1.0)
    keep_next = jnp.where(c == nchunks - 1, 0.0, 1.0)
    TE = T + 16

    for b in range(batch):
        x = zx_ref[b].astype(F32)
        xe = jnp.concatenate([zp_ref[b].astype(F32) * keep_prev, x, zn_ref[b].astype(F32) * keep_next], axis=0)
        xm2 = pltpu.roll(xe, 2, 0)[8:T + 8]
        xm1 = pltpu.roll(xe, 1, 0)[8:T + 8]
        xp1 = pltpu.roll(xe, TE - 1, 0)[8:T + 8]
        xr = cb + xm2 * cw[0:1] + xm1 * cw[1:2] + x * cw[2:3] + xp1 * cw[3:4]
        xb = xr.astype(BF16)
        ga = jnp.concatenate(
            [jnp.dot(xb[:, k * 128:(k + 1) * 128], w_ref[0, k], preferred_element_type=F32)
             for k in range(LRU_BLOCKS)], axis=-1)
        gi = jnp.concatenate(
            [jnp.dot(xb[:, k * 128:(k + 1) * 128], w_ref[1, k], preferred_element_type=F32)
             for k in range(LRU_BLOCKS)], axis=-1)
        a = jnp.exp2(half_coef2 * (1.0 + jnp.tanh(ga + half_ba)))
        i = 0.5 * (1.0 + jnp.tanh(gi + half_bi))
        y = 1.0 - a * a
        bx = jnp.where(y > 0.0, y * lax.rsqrt(y), 0.0) * (i * xr)
        for k in range(LRU_BLOCKS):
            a_scr[k, b * S:b * S + T, :] = a[:, k * 128:(k + 1) * 128]
            bx_scr[k, b * S:b * S + T, :] = bx[:, k * 128:(k + 1) * 128]

    def step(n, hs):
        tt = (T - 1 - n) if rev else n
        rows = pl.ds(tt, batch, stride=S)
        out = []
        for k in range(LRU_BLOCKS):
            h = a_scr[k, rows, :] * hs[k] + bx_scr[k, rows, :]
            bx_scr[k, rows, :] = h
            out.append(h)
        return tuple(out)

    h_init = tuple(hc_scr[:, k * 128:(k + 1) * 128] for k in range(LRU_BLOCKS))
    h_last = jnp.concatenate(lax.fori_loop(0, T, step, h_init, unroll=4), axis=-1)
    hc_scr[...] = h_last
    hfin_ref[...] = h_last

    for b in range(batch):
        hb = jnp.concatenate([bx_scr[k, b * S:b * S + T, :] for k in range(LRU_BLOCKS)], axis=-1)
        if rev:
            gate = jax.nn.gelu(zg_ref[b].astype(F32))
            out_ref[b] = ((hf_ref[b] + hb) * gate).astype(BF16)
        else:
            out_ref[b] = hb


def _lru_pass(z3, h0, cw, cb, w, bvec, lam, rev, hf=None):
    batch, L, _ = z3.shape
    T = min(LRU_T, L)
    nchunks = L // T
    nb8 = L // 8
    cidx = (lambda s: nchunks - 1 - s) if rev else (lambda s: s)
    main = lambda col: pl.BlockSpec((batch, T, LRU_WIDTH), lambda s: (0, cidx(s), col))
    prev_map = lambda s: (0, jnp.maximum(cidx(s) * (T // 8) - 1, 0), ZX_COL512)
    next_map = lambda s: (0, jnp.minimum((cidx(s) + 1) * (T // 8), nb8 - 1), ZX_COL512)
    const2 = lambda shape: pl.BlockSpec(shape, lambda s: (0, 0))
    in_specs = [main(ZX_COL512),
                pl.BlockSpec((batch, 8, LRU_WIDTH), prev_map),
                pl.BlockSpec((batch, 8, LRU_WIDTH), next_map),
                const2((batch, LRU_WIDTH)),
                const2((4, LRU_WIDTH)), const2((1, LRU_WIDTH)),
                pl.BlockSpec((2, LRU_BLOCKS, 128, 128), lambda s: (0, 0, 0, 0)),
                const2((2, LRU_WIDTH)), const2((1, LRU_WIDTH))]
    args = [z3, z3, z3, h0, cw, cb, w, bvec, lam]
    if rev:
        in_specs += [main(ZG_COL512), pl.BlockSpec((batch, T, LRU_WIDTH), lambda s: (0, cidx(s), 0))]
        args += [z3, hf]
    out_dtype = BF16 if rev else F32
    S = T + LRU_PAD
    return pl.pallas_call(
        functools.partial(_lru_kernel, T=T, nchunks=nchunks, rev=rev, batch=batch),
        grid=(nchunks,),
        in_specs=in_specs,
        out_specs=[pl.BlockSpec((batch, T, LRU_WIDTH), lambda s: (0, cidx(s), 0)),
                   const2((batch, LRU_WIDTH))],
        out_shape=[jax.ShapeDtypeStruct((batch, L, LRU_WIDTH), out_dtype),
                   jax.ShapeDtypeStruct((batch, LRU_WIDTH), F32)],
        scratch_shapes=[pltpu.VMEM((LRU_BLOCKS, batch * S, 128), F32),
                        pltpu.VMEM((LRU_BLOCKS, batch * S, 128), F32),
                        pltpu.VMEM((batch, LRU_WIDTH), F32)],
        compiler_params=_params(("arbitrary",)),
        name="lru_rev" if rev else "lru_fwd",
    )(*args)


def _out_proj_kernel(a_ref, m_ref, r_ref, x_ref, mod_ref, g_ref, w_ref, o_ref):
    mix = jnp.concatenate([a_ref[...], m_ref[...], r_ref[...]], axis=-1)
    y = jnp.dot(mix, w_ref[...], preferred_element_type=F32)
    o_ref[...] = _gated_norm_residual(x_ref[...], y, g_ref[...], mod_ref[0][2:3])


def _out_proj(attn, mlp, lru, x, mod, g, w, layer, rows_per_mod, tm):
    M = x.shape[0]
    if rows_per_mod is None:
        mod_map = lambda i: (8, 0, 0)
    else:
        blocks = rows_per_mod // tm
        mod_map = lambda i: (i // blocks, 0, 0)
    return pl.pallas_call(
        _out_proj_kernel,
        grid=(M // tm,),
        in_specs=[
            pl.BlockSpec((tm, ATTN_WIDTH), lambda i: (i, 0)),
            pl.BlockSpec((tm, MLP_WIDTH), lambda i: (i, 0)),
            pl.BlockSpec((tm, LRU_WIDTH), lambda i: (i, 0)),
            pl.BlockSpec((tm, D_MODEL), lambda i: (i, 0)),
            pl.BlockSpec((1, 6, D_MODEL), mod_map),
            pl.BlockSpec((1, D_MODEL), lambda i: (0, 0)),
            pl.BlockSpec((None, D_MODEL, D_MODEL), lambda i: (layer, 0, 0)),
        ],
        out_specs=pl.BlockSpec((tm, D_MODEL), lambda i: (i, 0)),
        out_shape=jax.ShapeDtypeStruct((M, D_MODEL), F32),
        compiler_params=_params(("arbitrary",)),
        name="out_proj",
    )(attn, mlp, lru, x, mod, g, w)


def _ffn_kernel(x_ref, mod_ref, g2_ref, g3_ref, wg_ref, wu_ref, wo_ref, o_ref, h_scr, acc_scr):
    j = pl.program_id(1)
    m = mod_ref[0]

    @pl.when(j == 0)
    def _():
        h = _norm_modulate(x_ref[...], g2_ref[...], m[3:4], m[4:5])
        h_scr[...] = h.astype(BF16)
        acc_scr[...] = jnp.zeros_like(acc_scr)

    h = h_scr[...]
    gate = jnp.dot(h, wg_ref[...], preferred_element_type=F32)
    up = jnp.dot(h, wu_ref[...], preferred_element_type=F32)
    act = (gate * jax.nn.sigmoid(gate) * up).astype(BF16)
    acc_scr[...] += jnp.dot(act, wo_ref[...], preferred_element_type=F32)

    @pl.when(j == pl.num_programs(1) - 1)
    def _():
        o_ref[...] = _gated_norm_residual(x_ref[...], acc_scr[...], g3_ref[...], m[5:6])


def _ffn(x, mod, g2, g3, w_in, w_out, layer, rows_per_mod, tm, tf):
    M = x.shape[0]
    nf = D_FF // tf
    if rows_per_mod is None:
        mod_map = lambda i, j: (8, 0, 0)
    else:
        blocks = rows_per_mod // tm
        mod_map = lambda i, j: (i // blocks, 0, 0)
    return pl.pallas_call(
        _ffn_kernel,
        grid=(M // tm, nf),
        in_specs=[
            pl.BlockSpec((tm, D_MODEL), lambda i, j: (i, 0)),
            pl.BlockSpec((1, 6, D_MODEL), mod_map),
            pl.BlockSpec((1, D_MODEL), lambda i, j: (0, 0)),
            pl.BlockSpec((1, D_MODEL), lambda i, j: (0, 0)),
            pl.BlockSpec((None, D_MODEL, tf), lambda i, j: (layer, 0, j)),
            pl.BlockSpec((None, D_MODEL, tf), lambda i, j: (layer, 0, j + nf)),
            pl.BlockSpec((None, tf, D_MODEL), lambda i, j: (layer, j, 0)),
        ],
        out_specs=pl.BlockSpec((tm, D_MODEL), lambda i, j: (i, 0)),
        out_shape=jax.ShapeDtypeStruct((M, D_MODEL), F32),
        scratch_shapes=[pltpu.VMEM((tm, D_MODEL), BF16), pltpu.VMEM((tm, D_MODEL), F32)],
        compiler_params=_params(("arbitrary", "arbitrary")),
        name="ffn",
    )(x, mod, g2, g3, w_in, w_in, w_out)


def _rope_tables(n_tokens):
    rows = n_tokens // GRID_W
    row_ids = jnp.repeat(jnp.arange(rows, dtype=F32), GRID_W)
    col_ids = jnp.tile(jnp.arange(GRID_W, dtype=F32), rows)
    inv_freq = ROPE_THETA ** (-jnp.arange(0, AXIS_DIM, 2, dtype=F32) / AXIS_DIM)
    ang_r = row_ids[:, None] * inv_freq
    ang_c = col_ids[:, None] * inv_freq
    ang = jnp.concatenate([ang_r, ang_r, ang_c, ang_c], axis=-1)
    sign = jnp.where((jnp.arange(HEAD_DIM) % AXIS_DIM) < (AXIS_DIM // 2), -1.0, 1.0).astype(F32)
    return jnp.cos(ang), jnp.sin(ang) * sign


def kernel(x, c, ctx, c_ctx, w_mod, b_mod, g_norm, w_in, g_qk, w_s, b_s, conv_w, conv_b,
           lru_w, lru_b, lru_lam, w_out, w_ffn_in, w_ffn_out):
    batch, seq, d = x.shape
    ctx_len = ctx.shape[1]
    depth = w_mod.shape[0]
    cos, sin_signed = _rope_tables(seq)

    c_all = jnp.concatenate([c, c_ctx[None, :], jnp.zeros((MOD_ROWS - batch - 1, d), F32)], axis=0)
    mod_all = _modulation(c_all, w_mod, b_mod).reshape(depth, MOD_ROWS, 6, d)

    xl = x.reshape(batch * seq, d)
    xc = ctx.reshape(batch * ctx_len, d)
    zeros_h = jnp.zeros((batch, LRU_WIDTH), F32)
    w_in_b = w_in.astype(BF16)
    w_out_b = w_out.astype(BF16)
    w_fi_b = w_ffn_in.astype(BF16)
    w_fo_b = w_ffn_out.astype(BF16)

    for l in range(depth):
        last = l == depth - 1
        mod = mod_all[l]
        g = g_norm[l].reshape(4, 1, d)
        ws = w_s[l].astype(BF16)
        bias = jnp.repeat(b_s[l].T, CHUNK, axis=1)
        lw = (0.5 * lru_w[l]).astype(BF16)
        cb = conv_b[l].reshape(1, LRU_WIDTH)

        zl = _in_proj(xl, mod, g[0], w_in_b, l, seq, 512)
        zc = _in_proj(xc, mod, g[0], w_in_b, l, None, 512)

        attn = _attention(zl, zc, g_qk[l], cos, sin_signed, batch, seq, ctx_len, True, 512)
        mlp = _chunk_mlp(zl, ws, bias, 512)

        zc3 = zc.reshape(batch, ctx_len, IN_COLS)
        zl3 = zl.reshape(batch, seq, IN_COLS)
        lru_args = lambda dd: (conv_w[l], cb, lw[dd], lru_b[l, dd], lru_lam[l, dd].reshape(1, LRU_WIDTH))
        hcf, h0f = _lru_pass(zc3, zeros_h, *lru_args(0), rev=False)
        lru_c, h0r = _lru_pass(zc3, zeros_h, *lru_args(1), rev=True, hf=hcf)
        hlf, _ = _lru_pass(zl3, h0f, *lru_args(0), rev=False)
        lru_l, _ = _lru_pass(zl3, h0r, *lru_args(1), rev=True, hf=hlf)

        xl = _out_proj(attn, mlp, lru_l.reshape(batch * seq, LRU_WIDTH), xl, mod, g[1], w_out_b, l, seq, 512)
        xl = _ffn(xl, mod, g[2], g[3], w_fi_b, w_fo_b, l, seq, 512, FFN_TF)

        if not last:
            attn_c = _attention(zc, zc, g_qk[l], cos, sin_signed, batch, ctx_len, ctx_len, False, ctx_len)
            mlp_c = _chunk_mlp(zc, ws, bias, 512)
            xc = _out_proj(attn_c, mlp_c, lru_c.reshape(batch * ctx_len, LRU_WIDTH), xc, mod, g[1], w_out_b,
                           l, None, 512)
            xc = _ffn(xc, mod, g[2], g[3], w_fi_b, w_fo_b, l, None, 512, FFN_TF)

    return xl.reshape(batch, seq, d)
```

```python
import functools

import jax
import jax.numpy as jnp
from jax import lax
from jax.experimental import pallas as pl
from jax.experimental.pallas import tpu as pltpu

F32 = jnp.float32
BF16 = jnp.bfloat16

D_MODEL = 2048
GRID_W = 64
EPS = 1e-6
HEAD_DIM = 128
N_Q_HEADS = 8
N_KV_HEADS = 2
Q_PER_KV = 4
ATTN_WIDTH = 1024
KV_WIDTH = 256
ROPE_THETA = 10000.0
AXIS_DIM = 64
CHUNK = 128
MLP_WIDTH = 512
MLP_GROUPS = 4
LRU_WIDTH = 512
LRU_BLOCKS = 4
LRU_C = 8.0
IN_COLS = 3584
D_FF = 5632

K_COL128 = ATTN_WIDTH // 128
V_COL128 = (ATTN_WIDTH + KV_WIDTH) // 128
ZM_COL512 = (ATTN_WIDTH + 2 * KV_WIDTH) // 512
ZX_COL512 = ZM_COL512 + 2
ZG_COL512 = ZX_COL512 + 1

VMEM_LIMIT = 56 * 1024 * 1024
MOD_ROWS = 16
LANES = 128
MLP_GROUP_W = MLP_WIDTH // MLP_GROUPS
LRU_BLOCK_W = LRU_WIDTH // LRU_BLOCKS
ROW_TILE = 512
ATTN_TQ = 512
ATTN_KC = 512
LOG2_E = 1.4426950408889634
FFN_TF = 512
LRU_T = 256
LRU_PAD = 8


def _params(sem):
    return pltpu.CompilerParams(dimension_semantics=sem, vmem_limit_bytes=VMEM_LIMIT)


def _unit_rms(xf):
    return xf * lax.rsqrt(jnp.mean(xf * xf, axis=-1, keepdims=True) + EPS)


def _norm_modulate(xf, g, shift, scale):
    return _unit_rms(xf) * (g * (1.0 + scale)) + shift


def _gated_norm_residual(x, y, g, gate):
    return x + _unit_rms(y) * (g * gate)


def _mod_kernel(c_ref, w_ref, b_ref, o_ref):
    c = c_ref[...]
    a = (c * jax.nn.sigmoid(c)).astype(BF16)
    o_ref[0] = jnp.dot(a, w_ref[0].astype(BF16), preferred_element_type=F32) + b_ref[0]


def _modulation(c_all, w_mod, b_mod):
    depth, d, n = w_mod.shape
    tn = 1024
    return pl.pallas_call(
        _mod_kernel,
        grid=(depth, n // tn),
        in_specs=[
            pl.BlockSpec((MOD_ROWS, d), lambda l, j: (0, 0)),
            pl.BlockSpec((1, d, tn), lambda l, j: (l, 0, j)),
            pl.BlockSpec((1, 1, tn), lambda l, j: (l, 0, j)),
        ],
        out_specs=pl.BlockSpec((1, MOD_ROWS, tn), lambda l, j: (l, 0, j)),
        out_shape=jax.ShapeDtypeStruct((depth, MOD_ROWS, n), F32),
        compiler_params=_params(("arbitrary", "arbitrary")),
        name="modulation",
    )(c_all, w_mod, b_mod.reshape(depth, 1, n))


def _in_proj_kernel(x_ref, mod_ref, g_ref, w_ref, z_ref):
    m = mod_ref[0]
    h = _norm_modulate(x_ref[...], g_ref[...], m[0:1], m[1:2])
    z_ref[...] = jnp.dot(h.astype(BF16), w_ref[...], preferred_element_type=F32).astype(BF16)


def _in_proj(x, mod, g, w, layer, rows_per_mod, tm):
    M = x.shape[0]
    if rows_per_mod is None:
        mod_map = lambda i: (8, 0, 0)
    else:
        blocks = rows_per_mod // tm
        mod_map = lambda i: (i // blocks, 0, 0)
    return pl.pallas_call(
        _in_proj_kernel,
        grid=(M // tm,),
        in_specs=[
            pl.BlockSpec((tm, D_MODEL), lambda i: (i, 0)),
            pl.BlockSpec((1, 6, D_MODEL), mod_map),
            pl.BlockSpec((1, D_MODEL), lambda i: (0, 0)),
            pl.BlockSpec((None, D_MODEL, IN_COLS), lambda i: (layer, 0, 0)),
        ],
        out_specs=pl.BlockSpec((tm, IN_COLS), lambda i: (i, 0)),
        out_shape=jax.ShapeDtypeStruct((M, IN_COLS), BF16),
        compiler_params=_params(("arbitrary",)),
        name="in_proj",
    )(x, mod, g, w)


def _head_norm(xf, g):
    return xf * lax.rsqrt(jnp.mean(xf * xf, axis=-1, keepdims=True) + EPS) * g


def _rope(xf, cos, sin_signed):
    lane = lax.broadcasted_iota(jnp.int32, xf.shape, 1)
    first = (lane % AXIS_DIM) < (AXIS_DIM // 2)
    partner = jnp.where(first, pltpu.roll(xf, HEAD_DIM - AXIS_DIM // 2, 1), pltpu.roll(xf, AXIS_DIM // 2, 1))
    return xf * cos + partner * sin_signed


_NT = (((1,), (1,)), ((), ()))


def _attn_kernel(*refs, tq, ctx_len, has_lat):
    if has_lat:
        (q_ref, kc_ref, vc_ref, kl_ref, vl_ref, cos_ref, sin_ref, gqk_ref, o_ref, kn_scr, vt_scr) = refs
    else:
        (q_ref, kc_ref, vc_ref, gqk_ref, o_ref, kn_scr, vt_scr) = refs
    qi = pl.program_id(2)
    gq = gqk_ref[0:1, :]
    gk = gqk_ref[1:2, :]
    n_keys = kn_scr.shape[0]

    @pl.when(qi == 0)
    def _():
        kn_scr[0:ctx_len, :] = _head_norm(kc_ref[...].astype(F32), gk).astype(BF16)
        vt_scr[:, 0:ctx_len] = vc_ref[...].astype(F32).T.astype(BF16)
        if has_lat:
            k = _head_norm(kl_ref[...].astype(F32), gk)
            kn_scr[ctx_len:, :] = _rope(k, cos_ref[...], sin_ref[...]).astype(BF16)
            vt_scr[:, ctx_len:] = vl_ref[...].astype(F32).T.astype(BF16)

    scale = HEAD_DIM ** -0.5 * LOG2_E
    heads = []
    for h in range(Q_PER_KV):
        qh = _head_norm(q_ref[:, h * HEAD_DIM:(h + 1) * HEAD_DIM].astype(F32), gq)
        if has_lat:
            rows = pl.ds(pl.multiple_of(qi * tq, tq), tq)
            qh = _rope(qh, cos_ref[rows, :], sin_ref[rows, :])
        heads.append((qh * scale).astype(BF16))

    qs = jnp.concatenate(heads, axis=0)

    s = lax.dot_general(kn_scr[...], qs, _NT, preferred_element_type=F32)
    m = jnp.max(s, axis=0, keepdims=True)
    l = jnp.zeros((1, Q_PER_KV * tq), F32)
    acc = jnp.zeros((HEAD_DIM, Q_PER_KV * tq), F32)
    bounds = [0, ctx_len] + list(range(ctx_len + ATTN_KC, n_keys + 1, ATTN_KC))
    for lo, hi in zip(bounds[:-1], bounds[1:]):
        rows = slice(lo, hi)
        p = jnp.exp2(s[rows] - m)
        l = l + jnp.sum(p, axis=0, keepdims=True)
        acc = acc + jnp.dot(vt_scr[:, rows], p.astype(BF16), preferred_element_type=F32)
    o = (acc / l).T
    for h in range(Q_PER_KV):
        o_ref[:, h * HEAD_DIM:(h + 1) * HEAD_DIM] = o[h * tq:(h + 1) * tq].astype(BF16)


def _attention(zq, zc, gqk, cos, sin_signed, batch, q_len, ctx_len, has_lat, tq):
    nq = q_len // tq
    n_keys = ctx_len + (q_len if has_lat else 0)
    kv_c = lambda col: pl.BlockSpec((ctx_len, HEAD_DIM), lambda b, k, i: (b, col + k))
    kv_l = lambda col: pl.BlockSpec((q_len, HEAD_DIM), lambda b, k, i: (b, col + k))
    in_specs = [pl.BlockSpec((tq, Q_PER_KV * HEAD_DIM), lambda b, k, i: (b * nq + i, k)),
                kv_c(K_COL128), kv_c(V_COL128)]
    args = [zq, zc, zc]
    scratch = [pltpu.VMEM((n_keys, HEAD_DIM), BF16), pltpu.VMEM((HEAD_DIM, n_keys), BF16)]
    if has_lat:
        in_specs += [kv_l(K_COL128), kv_l(V_COL128),
                     pl.BlockSpec((q_len, HEAD_DIM), lambda b, k, i: (0, 0)),
                     pl.BlockSpec((q_len, HEAD_DIM), lambda b, k, i: (0, 0))]
        args += [zq, zq, cos, sin_signed]
    in_specs.append(pl.BlockSpec((2, HEAD_DIM), lambda b, k, i: (0, 0)))
    args.append(gqk)
    return pl.pallas_call(
        functools.partial(_attn_kernel, tq=tq, ctx_len=ctx_len, has_lat=has_lat),
        grid=(batch, N_KV_HEADS, nq),
        in_specs=in_specs,
        out_specs=pl.BlockSpec((tq, Q_PER_KV * HEAD_DIM), lambda b, k, i: (b * nq + i, k)),
        out_shape=jax.ShapeDtypeStruct((batch * q_len, ATTN_WIDTH), BF16),
        scratch_shapes=scratch,
        compiler_params=_params(("arbitrary", "arbitrary", "arbitrary")),
        name="attention_lat" if has_lat else "attention_ctx",
    )(*args)


def _cmlp_kernel(zu_ref, zv_ref, ws_ref, bs_ref, o_ref, *, tm):
    bias = bs_ref[...]
    for n in range(tm // CHUNK):
        rows = slice(n * CHUNK, (n + 1) * CHUNK)
        v = jax.nn.gelu(zv_ref[rows, :].astype(F32)).astype(BF16)
        parts = [jnp.dot(ws_ref[g], v[:, g * MLP_GROUP_W:(g + 1) * MLP_GROUP_W], preferred_element_type=F32)
                 for g in range(MLP_GROUPS)]
        s = jnp.concatenate(parts, axis=-1) + bias
        u = jax.nn.gelu(zu_ref[rows, :].astype(F32))
        o_ref[rows, :] = (u * s).astype(BF16)


def _chunk_mlp(z, ws, bias, tm):
    M = z.shape[0]
    return pl.pallas_call(
        functools.partial(_cmlp_kernel, tm=tm),
        grid=(M // tm,),
        in_specs=[
            pl.BlockSpec((tm, MLP_WIDTH), lambda i: (i, ZM_COL512)),
            pl.BlockSpec((tm, MLP_WIDTH), lambda i: (i, ZM_COL512 + 1)),
            pl.BlockSpec((MLP_GROUPS, CHUNK, CHUNK), lambda i: (0, 0, 0)),
            pl.BlockSpec((CHUNK, MLP_WIDTH), lambda i: (0, 0)),
        ],
        out_specs=pl.BlockSpec((tm, MLP_WIDTH), lambda i: (i, 0)),
        out_shape=jax.ShapeDtypeStruct((M, MLP_WIDTH), BF16),
        compiler_params=_params(("arbitrary",)),
        name="chunk_mlp",
    )(z, z, ws, bias)


def _lru_kernel(*refs, T, nchunks, rev, batch):
    if rev:
        (zx_ref, zp_ref, zn_ref, h0_ref, cw_ref, cb_ref, w_ref, b_ref, lam_ref, zg_ref, hf_ref,
         out_ref, hfin_ref, a_scr, bx_scr, hc_scr) = refs
    else:
        (zx_ref, zp_ref, zn_ref, h0_ref, cw_ref, cb_ref, w_ref, b_ref, lam_ref,
         out_ref, hfin_ref, a_scr, bx_scr, hc_scr) = refs
    S = T + LRU_PAD
    s = pl.program_id(0)
    c = (nchunks - 1 - s) if rev else s

    @pl.when(s == 0)
    def _():
        hc_scr[...] = h0_ref[...]

    cw = cw_ref[...]
    cb = cb_ref[...]
    lam = lam_ref[...]
    neg_lam = -lam
    softplus = jnp.maximum(neg_lam, 0.0) + jnp.log1p(jnp.exp(-jnp.abs(neg_lam)))
    half_coef2 = (-0.5 * LRU_C * LOG2_E) * softplus
    half_ba = 0.5 * b_ref[0:1, :]
    half_bi = 0.5 * b_ref[1:2, :]
    at_start = c == 0
    at_end = c == nchunks - 1
    TE = T + 16

    for b in range(batch):
        x = zx_ref[b].astype(F32)
        prev = jnp.where(at_start, 0.0, zp_ref[b].astype(F32))
        nxt = jnp.where(at_end, 0.0, zn_ref[b].astype(F32))
        xe = jnp.concatenate([prev, x, nxt], axis=0)
        xm2 = pltpu.roll(xe, 2, 0)[8:T + 8]
        xm1 = pltpu.roll(xe, 1, 0)[8:T + 8]
        xp1 = pltpu.roll(xe, TE - 1, 0)[8:T + 8]
        xr = cb + xm2 * cw[0:1] + xm1 * cw[1:2] + x * cw[2:3] + xp1 * cw[3:4]
        xb = xr.astype(BF16)
        ga = jnp.concatenate(
            [jnp.dot(xb[:, k * LANES:(k + 1) * LANES], w_ref[0, k], preferred_element_type=F32)
             for k in range(LRU_BLOCKS)], axis=-1)
        gi = jnp.concatenate(
            [jnp.dot(xb[:, k * LANES:(k + 1) * LANES], w_ref[1, k], preferred_element_type=F32)
             for k in range(LRU_BLOCKS)], axis=-1)
        a = jnp.exp2(half_coef2 * (1.0 + jnp.tanh(ga + half_ba)))
        i = 0.5 * (1.0 + jnp.tanh(gi + half_bi))
        y = 1.0 - a * a
        bx = jnp.where(y > 0.0, y * lax.rsqrt(y), 0.0) * (i * xr)
        for k in range(LRU_BLOCKS):
            a_scr[k, b * S:b * S + T, :] = a[:, k * LANES:(k + 1) * LANES]
            bx_scr[k, b * S:b * S + T, :] = bx[:, k * LANES:(k + 1) * LANES]

    def step(n, hs):
        tt = (T - 1 - n) if rev else n
        rows = pl.ds(tt, batch, stride=S)
        out = []
        for k in range(LRU_BLOCKS):
            h = a_scr[k, rows, :] * hs[k] + bx_scr[k, rows, :]
            bx_scr[k, rows, :] = h
            out.append(h)
        return tuple(out)

    h_init = tuple(hc_scr[:, k * LANES:(k + 1) * LANES] for k in range(LRU_BLOCKS))
    h_last = jnp.concatenate(lax.fori_loop(0, T, step, h_init, unroll=4), axis=-1)
    hc_scr[...] = h_last
    hfin_ref[...] = h_last

    for b in range(batch):
        hb = jnp.concatenate([bx_scr[k, b * S:b * S + T, :] for k in range(LRU_BLOCKS)], axis=-1)
        if rev:
            gate = jax.nn.gelu(zg_ref[b].astype(F32))
            out_ref[b] = ((hf_ref[b] + hb) * gate).astype(BF16)
        else:
            out_ref[b] = hb


def _lru_pass(z3, h0, cw, cb, w, bvec, lam, rev, hf=None):
    batch, L, _ = z3.shape
    T = min(LRU_T, L)
    nchunks = L // T
    nb8 = L // 8
    cidx = (lambda s: nchunks - 1 - s) if rev else (lambda s: s)
    main = lambda col: pl.BlockSpec((batch, T, LRU_WIDTH), lambda s: (0, cidx(s), col))
    prev_map = lambda s: (0, jnp.maximum(cidx(s) * (T // 8) - 1, 0), ZX_COL512)
    next_map = lambda s: (0, jnp.minimum((cidx(s) + 1) * (T // 8), nb8 - 1), ZX_COL512)
    const2 = lambda shape: pl.BlockSpec(shape, lambda s: (0, 0))
    in_specs = [main(ZX_COL512),
                pl.BlockSpec((batch, 8, LRU_WIDTH), prev_map),
                pl.BlockSpec((batch, 8, LRU_WIDTH), next_map),
                const2((batch, LRU_WIDTH)),
                const2((4, LRU_WIDTH)), const2((1, LRU_WIDTH)),
                pl.BlockSpec((2, LRU_BLOCKS, LRU_BLOCK_W, LRU_BLOCK_W), lambda s: (0, 0, 0, 0)),
                const2((2, LRU_WIDTH)), const2((1, LRU_WIDTH))]
    args = [z3, z3, z3, h0, cw, cb, w, bvec, lam]
    if rev:
        in_specs += [main(ZG_COL512), pl.BlockSpec((batch, T, LRU_WIDTH), lambda s: (0, cidx(s), 0))]
        args += [z3, hf]
    out_dtype = BF16 if rev else F32
    S = T + LRU_PAD
    return pl.pallas_call(
        functools.partial(_lru_kernel, T=T, nchunks=nchunks, rev=rev, batch=batch),
        grid=(nchunks,),
        in_specs=in_specs,
        out_specs=[pl.BlockSpec((batch, T, LRU_WIDTH), lambda s: (0, cidx(s), 0)),
                   const2((batch, LRU_WIDTH))],
        out_shape=[jax.ShapeDtypeStruct((batch, L, LRU_WIDTH), out_dtype),
                   jax.ShapeDtypeStruct((batch, LRU_WIDTH), F32)],
        scratch_shapes=[pltpu.VMEM((LRU_BLOCKS, batch * S, LANES), F32),
                        pltpu.VMEM((LRU_BLOCKS, batch * S, LANES), F32),
                        pltpu.VMEM((batch, LRU_WIDTH), F32)],
        compiler_params=_params(("arbitrary",)),
        name="lru_rev" if rev else "lru_fwd",
    )(*args)


def _out_proj_kernel(a_ref, m_ref, r_ref, x_ref, mod_ref, g_ref, w_ref, o_ref):
    mix = jnp.concatenate([a_ref[...], m_ref[...], r_ref[...]], axis=-1)
    y = jnp.dot(mix, w_ref[...], preferred_element_type=F32)
    o_ref[...] = _gated_norm_residual(x_ref[...], y, g_ref[...], mod_ref[0][2:3])


def _out_proj(attn, mlp, lru, x, mod, g, w, layer, rows_per_mod, tm):
    M = x.shape[0]
    if rows_per_mod is None:
        mod_map = lambda i: (8, 0, 0)
    else:
        blocks = rows_per_mod // tm
        mod_map = lambda i: (i // blocks, 0, 0)
    return pl.pallas_call(
        _out_proj_kernel,
        grid=(M // tm,),
        in_specs=[
            pl.BlockSpec((tm, ATTN_WIDTH), lambda i: (i, 0)),
            pl.BlockSpec((tm, MLP_WIDTH), lambda i: (i, 0)),
            pl.BlockSpec((tm, LRU_WIDTH), lambda i: (i, 0)),
            pl.BlockSpec((tm, D_MODEL), lambda i: (i, 0)),
            pl.BlockSpec((1, 6, D_MODEL), mod_map),
            pl.BlockSpec((1, D_MODEL), lambda i: (0, 0)),
            pl.BlockSpec((None, D_MODEL, D_MODEL), lambda i: (layer, 0, 0)),
        ],
        out_specs=pl.BlockSpec((tm, D_MODEL), lambda i: (i, 0)),
        out_shape=jax.ShapeDtypeStruct((M, D_MODEL), F32),
        compiler_params=_params(("arbitrary",)),
        name="out_proj",
    )(attn, mlp, lru, x, mod, g, w)


def _ffn_kernel(x_ref, mod_ref, g2_ref, g3_ref, wg_ref, wu_ref, wo_ref, o_ref, h_scr, acc_scr):
    j = pl.program_id(1)
    m = mod_ref[0]

    @pl.when(j == 0)
    def _():
        h = _norm_modulate(x_ref[...], g2_ref[...], m[3:4], m[4:5])
        h_scr[...] = h.astype(BF16)
        acc_scr[...] = jnp.zeros_like(acc_scr)

    h = h_scr[...]
    gate = jnp.dot(h, wg_ref[...], preferred_element_type=F32)
    up = jnp.dot(h, wu_ref[...], preferred_element_type=F32)
    act = (gate * jax.nn.sigmoid(gate) * up).astype(BF16)
    acc_scr[...] += jnp.dot(act, wo_ref[...], preferred_element_type=F32)

    @pl.when(j == pl.num_programs(1) - 1)
    def _():
        o_ref[...] = _gated_norm_residual(x_ref[...], acc_scr[...], g3_ref[...], m[5:6])


def _ffn(x, mod, g2, g3, w_in, w_out, layer, rows_per_mod, tm, tf):
    M = x.shape[0]
    nf = D_FF // tf
    if rows_per_mod is None:
        mod_map = lambda i, j: (8, 0, 0)
    else:
        blocks = rows_per_mod // tm
        mod_map = lambda i, j: (i // blocks, 0, 0)
    return pl.pallas_call(
        _ffn_kernel,
        grid=(M // tm, nf),
        in_specs=[
            pl.BlockSpec((tm, D_MODEL), lambda i, j: (i, 0)),
            pl.BlockSpec((1, 6, D_MODEL), mod_map),
            pl.BlockSpec((1, D_MODEL), lambda i, j: (0, 0)),
            pl.BlockSpec((1, D_MODEL), lambda i, j: (0, 0)),
            pl.BlockSpec((None, D_MODEL, tf), lambda i, j: (layer, 0, j)),
            pl.BlockSpec((None, D_MODEL, tf), lambda i, j: (layer, 0, j + nf)),
            pl.BlockSpec((None, tf, D_MODEL), lambda i, j: (layer, j, 0)),
        ],
        out_specs=pl.BlockSpec((tm, D_MODEL), lambda i, j: (i, 0)),
        out_shape=jax.ShapeDtypeStruct((M, D_MODEL), F32),
        scratch_shapes=[pltpu.VMEM((tm, D_MODEL), BF16), pltpu.VMEM((tm, D_MODEL), F32)],
        compiler_params=_params(("arbitrary", "arbitrary")),
        name="ffn",
    )(x, mod, g2, g3, w_in, w_in, w_out)


def _rope_tables(n_tokens):
    rows = n_tokens // GRID_W
    row_ids = jnp.repeat(jnp.arange(rows, dtype=F32), GRID_W)
    col_ids = jnp.tile(jnp.arange(GRID_W, dtype=F32), rows)
    inv_freq = ROPE_THETA ** (-jnp.arange(0, AXIS_DIM, 2, dtype=F32) / AXIS_DIM)
    ang_r = row_ids[:, None] * inv_freq
    ang_c = col_ids[:, None] * inv_freq
    ang = jnp.concatenate([ang_r, ang_r, ang_c, ang_c], axis=-1)
    sign = jnp.where((jnp.arange(HEAD_DIM) % AXIS_DIM) < (AXIS_DIM // 2), -1.0, 1.0).astype(F32)
    return jnp.cos(ang), jnp.sin(ang) * sign


def kernel(x, c, ctx, c_ctx, w_mod, b_mod, g_norm, w_in, g_qk, w_s, b_s, conv_w, conv_b,
           lru_w, lru_b, lru_lam, w_out, w_ffn_in, w_ffn_out):
    batch, seq, d = x.shape
    ctx_len = ctx.shape[1]
    depth = w_mod.shape[0]
    cos, sin_signed = _rope_tables(seq)

    c_all = jnp.concatenate([c, c_ctx[None, :], jnp.zeros((MOD_ROWS - batch - 1, d), F32)], axis=0)
    mod_all = _modulation(c_all, w_mod, b_mod).reshape(depth, MOD_ROWS, 6, d)

    xl = x.reshape(batch * seq, d)
    xc = ctx.reshape(batch * ctx_len, d)
    zeros_h = jnp.zeros((batch, LRU_WIDTH), F32)
    w_in_b = w_in.astype(BF16)
    w_out_b = w_out.astype(BF16)
    w_fi_b = w_ffn_in.astype(BF16)
    w_fo_b = w_ffn_out.astype(BF16)

    for l in range(depth):
        last = l == depth - 1
        mod = mod_all[l]
        g = g_norm[l].reshape(4, 1, d)
        ws = w_s[l].astype(BF16)
        bias = jnp.repeat(b_s[l].T, CHUNK, axis=1)
        lw = (0.5 * lru_w[l]).astype(BF16)
        cb = conv_b[l].reshape(1, LRU_WIDTH)

        zl = _in_proj(xl, mod, g[0], w_in_b, l, seq, ROW_TILE)
        zc = _in_proj(xc, mod, g[0], w_in_b, l, None, ROW_TILE)

        attn = _attention(zl, zc, g_qk[l], cos, sin_signed, batch, seq, ctx_len, True, ATTN_TQ)
        mlp = _chunk_mlp(zl, ws, bias, ROW_TILE)

        zc3 = zc.reshape(batch, ctx_len, IN_COLS)
        zl3 = zl.reshape(batch, seq, IN_COLS)
        lru_args = lambda dd: (conv_w[l], cb, lw[dd], lru_b[l, dd], lru_lam[l, dd].reshape(1, LRU_WIDTH))
        hcf, h0f = _lru_pass(zc3, zeros_h, *lru_args(0), rev=False)
        lru_c, h0r = _lru_pass(zc3, zeros_h, *lru_args(1), rev=True, hf=hcf)
        hlf, _ = _lru_pass(zl3, h0f, *lru_args(0), rev=False)
        lru_l, _ = _lru_pass(zl3, h0r, *lru_args(1), rev=True, hf=hlf)

        xl = _out_proj(attn, mlp, lru_l.reshape(batch * seq, LRU_WIDTH), xl, mod, g[1], w_out_b, l, seq,
                       ROW_TILE)
        xl = _ffn(xl, mod, g[2], g[3], w_fi_b, w_fo_b, l, seq, ROW_TILE, FFN_TF)

        if not last:
            attn_c = _attention(zc, zc, g_qk[l], cos, sin_signed, batch, ctx_len, ctx_len, False, ctx_len)
            mlp_c = _chunk_mlp(zc, ws, bias, ROW_TILE)
            xc = _out_proj(attn_c, mlp_c, lru_c.reshape(batch * ctx_len, LRU_WIDTH), xc, mod, g[1], w_out_b,
                           l, None, ROW_TILE)
            xc = _ffn(xc, mod, g[2], g[3], w_fi_b, w_fo_b, l, None, ROW_TILE, FFN_TF)

    return xl.reshape(batch, seq, d)
```

```python
import functools

import jax
import jax.numpy as jnp
from jax import lax
from jax.experimental import pallas as pl
from jax.experimental.pallas import tpu as pltpu

F32 = jnp.float32
BF16 = jnp.bfloat16

D_MODEL = 2048
GRID_W = 64
EPS = 1e-6
HEAD_DIM = 128
N_Q_HEADS = 8
N_KV_HEADS = 2
Q_PER_KV = 4
ATTN_WIDTH = 1024
KV_WIDTH = 256
ROPE_THETA = 10000.0
AXIS_DIM = 64
CHUNK = 128
MLP_WIDTH = 512
MLP_GROUPS = 4
LRU_WIDTH = 512
LRU_BLOCKS = 4
LRU_C = 8.0
IN_COLS = 3584
D_FF = 5632

K_COL128 = ATTN_WIDTH // 128
V_COL128 = (ATTN_WIDTH + KV_WIDTH) // 128
ZM_COL512 = (ATTN_WIDTH + 2 * KV_WIDTH) // 512
ZX_COL512 = ZM_COL512 + 2
ZG_COL512 = ZX_COL512 + 1

VMEM_LIMIT = 56 * 1024 * 1024
MOD_ROWS = 16
LANES = 128
MLP_GROUP_W = MLP_WIDTH // MLP_GROUPS
LRU_BLOCK_W = LRU_WIDTH // LRU_BLOCKS
ROW_TILE = 512
ATTN_TQ = 512
ATTN_KC = 512
LOG2_E = 1.4426950408889634
FFN_TF = 512
LRU_T = 256
LRU_PAD = 8


def _params(sem):
    return pltpu.CompilerParams(dimension_semantics=sem, vmem_limit_bytes=VMEM_LIMIT)


def _unit_rms(xf):
    return xf * lax.rsqrt(jnp.mean(xf * xf, axis=-1, keepdims=True) + EPS)


def _norm_modulate(xf, g, shift, scale):
    return _unit_rms(xf) * (g * (1.0 + scale)) + shift


def _gated_norm_residual(x, y, g, gate):
    return x + _unit_rms(y) * (g * gate)


def _mod_kernel(c_ref, w_ref, b_ref, o_ref):
    c = c_ref[...]
    a = (c * jax.nn.sigmoid(c)).astype(BF16)
    o_ref[0] = jnp.dot(a, w_ref[0].astype(BF16), preferred_element_type=F32) + b_ref[0]


def _modulation(c_all, w_mod, b_mod):
    depth, d, n = w_mod.shape
    tn = 1024
    return pl.pallas_call(
        _mod_kernel,
        grid=(depth, n // tn),
        in_specs=[
            pl.BlockSpec((MOD_ROWS, d), lambda l, j: (0, 0)),
            pl.BlockSpec((1, d, tn), lambda l, j: (l, 0, j)),
            pl.BlockSpec((1, 1, tn), lambda l, j: (l, 0, j)),
        ],
        out_specs=pl.BlockSpec((1, MOD_ROWS, tn), lambda l, j: (l, 0, j)),
        out_shape=jax.ShapeDtypeStruct((depth, MOD_ROWS, n), F32),
        compiler_params=_params(("arbitrary", "arbitrary")),
        name="modulation",
    )(c_all, w_mod, b_mod.reshape(depth, 1, n))


def _in_proj_kernel(x_ref, mod_ref, g_ref, w_ref, z_ref):
    m = mod_ref[0]
    h = _norm_modulate(x_ref[...], g_ref[...], m[0:1], m[1:2])
    z_ref[...] = jnp.dot(h.astype(BF16), w_ref[...], preferred_element_type=F32).astype(BF16)


def _in_proj(x, mod, g, w, layer, rows_per_mod, tm):
    M = x.shape[0]
    if rows_per_mod is None:
        mod_map = lambda i: (8, 0, 0)
    else:
        blocks = rows_per_mod // tm
        mod_map = lambda i: (i // blocks, 0, 0)
    return pl.pallas_call(
        _in_proj_kernel,
        grid=(M // tm,),
        in_specs=[
            pl.BlockSpec((tm, D_MODEL), lambda i: (i, 0)),
            pl.BlockSpec((1, 6, D_MODEL), mod_map),
            pl.BlockSpec((1, D_MODEL), lambda i: (0, 0)),
            pl.BlockSpec((None, D_MODEL, IN_COLS), lambda i: (layer, 0, 0)),
        ],
        out_specs=pl.BlockSpec((tm, IN_COLS), lambda i: (i, 0)),
        out_shape=jax.ShapeDtypeStruct((M, IN_COLS), BF16),
        compiler_params=_params(("arbitrary",)),
        name="in_proj",
    )(x, mod, g, w)


def _head_norm(xf, g):
    return xf * lax.rsqrt(jnp.mean(xf * xf, axis=-1, keepdims=True) + EPS) * g


def _rope(xf, cos, sin_signed):
    lane = lax.broadcasted_iota(jnp.int32, xf.shape, 1)
    first = (lane % AXIS_DIM) < (AXIS_DIM // 2)
    partner = jnp.where(first, pltpu.roll(xf, HEAD_DIM - AXIS_DIM // 2, 1), pltpu.roll(xf, AXIS_DIM // 2, 1))
    return xf * cos + partner * sin_signed


_NT = (((1,), (1,)), ((), ()))


def _attn_kernel(*refs, tq, ctx_len, has_lat):
    if has_lat:
        (q_ref, kc_ref, vc_ref, kl_ref, vl_ref, cos_ref, sin_ref, gqk_ref, o_ref, kn_scr, vt_scr) = refs
    else:
        (q_ref, kc_ref, vc_ref, gqk_ref, o_ref, kn_scr, vt_scr) = refs
    qi = pl.program_id(2)
    gq = gqk_ref[0:1, :]
    gk = gqk_ref[1:2, :]
    n_keys = kn_scr.shape[0]

    @pl.when(qi == 0)
    def _():
        kn_scr[0:ctx_len, :] = _head_norm(kc_ref[...].astype(F32), gk).astype(BF16)
        vt_scr[:, 0:ctx_len] = vc_ref[...].astype(F32).T.astype(BF16)
        if has_lat:
            k = _head_norm(kl_ref[...].astype(F32), gk)
            kn_scr[ctx_len:, :] = _rope(k, cos_ref[...], sin_ref[...]).astype(BF16)
            vt_scr[:, ctx_len:] = vl_ref[...].astype(F32).T.astype(BF16)

    scale = HEAD_DIM ** -0.5 * LOG2_E
    heads = []
    for h in range(Q_PER_KV):
        qh = _head_norm(q_ref[:, h * HEAD_DIM:(h + 1) * HEAD_DIM].astype(F32), gq)
        if has_lat:
            rows = pl.ds(pl.multiple_of(qi * tq, tq), tq)
            qh = _rope(qh, cos_ref[rows, :], sin_ref[rows, :])
        heads.append((qh * scale).astype(BF16))

    qs = jnp.concatenate(heads, axis=0)

    s = lax.dot_general(kn_scr[...], qs, _NT, preferred_element_type=F32)
    m = jnp.max(s, axis=0, keepdims=True)
    l = jnp.zeros((1, Q_PER_KV * tq), F32)
    acc = jnp.zeros((HEAD_DIM, Q_PER_KV * tq), F32)
    bounds = [0, ctx_len] + list(range(ctx_len + ATTN_KC, n_keys + 1, ATTN_KC))
    for lo, hi in zip(bounds[:-1], bounds[1:]):
        rows = slice(lo, hi)
        p = jnp.exp2(s[rows] - m)
        l = l + jnp.sum(p, axis=0, keepdims=True)
        acc = acc + jnp.dot(vt_scr[:, rows], p.astype(BF16), preferred_element_type=F32)
    o = (acc / l).T
    for h in range(Q_PER_KV):
        o_ref[:, h * HEAD_DIM:(h + 1) * HEAD_DIM] = o[h * tq:(h + 1) * tq].astype(BF16)


def _attention(zq, zc, gqk, cos, sin_signed, batch, q_len, ctx_len, has_lat, tq):
    nq = q_len // tq
    n_keys = ctx_len + (q_len if has_lat else 0)
    kv_c = lambda col: pl.BlockSpec((ctx_len, HEAD_DIM), lambda b, k, i: (b, col + k))
    kv_l = lambda col: pl.BlockSpec((q_len, HEAD_DIM), lambda b, k, i: (b, col + k))
    in_specs = [pl.BlockSpec((tq, Q_PER_KV * HEAD_DIM), lambda b, k, i: (b * nq + i, k)),
                kv_c(K_COL128), kv_c(V_COL128)]
    args = [zq, zc, zc]
    scratch = [pltpu.VMEM((n_keys, HEAD_DIM), BF16), pltpu.VMEM((HEAD_DIM, n_keys), BF16)]
    if has_lat:
        in_specs += [kv_l(K_COL128), kv_l(V_COL128),
                     pl.BlockSpec((q_len, HEAD_DIM), lambda b, k, i: (0, 0)),
                     pl.BlockSpec((q_len, HEAD_DIM), lambda b, k, i: (0, 0))]
        args += [zq, zq, cos, sin_signed]
    in_specs.append(pl.BlockSpec((2, HEAD_DIM), lambda b, k, i: (0, 0)))
    args.append(gqk)
    return pl.pallas_call(
        functools.partial(_attn_kernel, tq=tq, ctx_len=ctx_len, has_lat=has_lat),
        grid=(batch, N_KV_HEADS, nq),
        in_specs=in_specs,
        out_specs=pl.BlockSpec((tq, Q_PER_KV * HEAD_DIM), lambda b, k, i: (b * nq + i, k)),
        out_shape=jax.ShapeDtypeStruct((batch * q_len, ATTN_WIDTH), BF16),
        scratch_shapes=scratch,
        compiler_params=_params(("arbitrary", "arbitrary", "arbitrary")),
        name="attention_lat" if has_lat else "attention_ctx",
    )(*args)


def _cmlp_kernel(zu_ref, zv_ref, ws_ref, bs_ref, o_ref, *, tm):
    bias = bs_ref[...]
    for n in range(tm // CHUNK):
        rows = slice(n * CHUNK, (n + 1) * CHUNK)
        v = jax.nn.gelu(zv_ref[rows, :].astype(F32)).astype(BF16)
        parts = [jnp.dot(ws_ref[g], v[:, g * MLP_GROUP_W:(g + 1) * MLP_GROUP_W], preferred_element_type=F32)
                 for g in range(MLP_GROUPS)]
        s = jnp.concatenate(parts, axis=-1) + bias
        u = jax.nn.gelu(zu_ref[rows, :].astype(F32))
        o_ref[rows, :] = (u * s).astype(BF16)


def _chunk_mlp(z, ws, bias, tm):
    M = z.shape[0]
    return pl.pallas_call(
        functools.partial(_cmlp_kernel, tm=tm),
        grid=(M // tm,),
        in_specs=[
            pl.BlockSpec((tm, MLP_WIDTH), lambda i: (i, ZM_COL512)),
            pl.BlockSpec((tm, MLP_WIDTH), lambda i: (i, ZM_COL512 + 1)),
            pl.BlockSpec((MLP_GROUPS, CHUNK, CHUNK), lambda i: (0, 0, 0)),
            pl.BlockSpec((CHUNK, MLP_WIDTH), lambda i: (0, 0)),
        ],
        out_specs=pl.BlockSpec((tm, MLP_WIDTH), lambda i: (i, 0)),
        out_shape=jax.ShapeDtypeStruct((M, MLP_WIDTH), BF16),
        compiler_params=_params(("arbitrary",)),
        name="chunk_mlp",
    )(z, z, ws, bias)


def _lru_kernel(*refs, T, nchunks, rev, batch):
    if rev:
        (zx_ref, zp_ref, zn_ref, h0_ref, cw_ref, cb_ref, w_ref, b_ref, lam_ref, zg_ref, hf_ref,
         out_ref, hfin_ref, a_scr, bx_scr, hc_scr) = refs
    else:
        (zx_ref, zp_ref, zn_ref, h0_ref, cw_ref, cb_ref, w_ref, b_ref, lam_ref,
         out_ref, hfin_ref, a_scr, bx_scr, hc_scr) = refs
    S = T + LRU_PAD
    s = pl.program_id(0)
    c = (nchunks - 1 - s) if rev else s

    @pl.when(s == 0)
    def _():
        hc_scr[...] = h0_ref[...]

    cw = cw_ref[...]
    cb = cb_ref[...]
    lam = lam_ref[...]
    neg_lam = -lam
    softplus = jnp.maximum(neg_lam, 0.0) + jnp.log1p(jnp.exp(-jnp.abs(neg_lam)))
    half_coef2 = (-0.5 * LRU_C * LOG2_E) * softplus
    half_ba = 0.5 * b_ref[0:1, :]
    half_bi = 0.5 * b_ref[1:2, :]
    at_start = c == 0
    at_end = c == nchunks - 1
    TE = T + 16

    for b in range(batch):
        x = zx_ref[b].astype(F32)
        prev = jnp.where(at_start, 0.0, zp_ref[b].astype(F32))
        nxt = jnp.where(at_end, 0.0, zn_ref[b].astype(F32))
        xe = jnp.concatenate([prev, x, nxt], axis=0)
        xm2 = pltpu.roll(xe, 2, 0)[8:T + 8]
        xm1 = pltpu.roll(xe, 1, 0)[8:T + 8]
        xp1 = pltpu.roll(xe, TE - 1, 0)[8:T + 8]
        xr = cb + xm2 * cw[0:1] + xm1 * cw[1:2] + x * cw[2:3] + xp1 * cw[3:4]
        xb = xr.astype(BF16)
        ga = jnp.concatenate(
            [jnp.dot(xb[:, k * LANES:(k + 1) * LANES], w_ref[0, k], preferred_element_type=F32)
             for k in range(LRU_BLOCKS)], axis=-1)
        gi = jnp.concatenate(
            [jnp.dot(xb[:, k * LANES:(k + 1) * LANES], w_ref[1, k], preferred_element_type=F32)
             for k in range(LRU_BLOCKS)], axis=-1)
        a = jnp.exp2(half_coef2 * (1.0 + jnp.tanh(ga + half_ba)))
        i = 0.5 * (1.0 + jnp.tanh(gi + half_bi))
        y = 1.0 - a * a
        bx = jnp.where(y > 0.0, y * lax.rsqrt(y), 0.0) * (i * xr)
        for k in range(LRU_BLOCKS):
            a_scr[k, b * S:b * S + T, :] = a[:, k * LANES:(k + 1) * LANES]
            bx_scr[k, b * S:b * S + T, :] = bx[:, k * LANES:(k + 1) * LANES]

    def step(n, hs):
        tt = (T - 1 - n) if rev else n
        rows = pl.ds(tt, batch, stride=S)
        out = []
        for k in range(LRU_BLOCKS):
            h = a_scr[k, rows, :] * hs[k] + bx_scr[k, rows, :]
            bx_scr[k, rows, :] = h
            out.append(h)
        return tuple(out)

    h_init = tuple(hc_scr[:, k * LANES:(k + 1) * LANES] for k in range(LRU_BLOCKS))
    h_last = jnp.concatenate(lax.fori_loop(0, T, step, h_init, unroll=4), axis=-1)
    hc_scr[...] = h_last
    hfin_ref[...] = h_last

    for b in range(batch):
        hb = jnp.concatenate([bx_scr[k, b * S:b * S + T, :] for k in range(LRU_BLOCKS)], axis=-1)
        if rev:
            gate = jax.nn.gelu(zg_ref[b].astype(F32))
            out_ref[b] = ((hf_ref[b] + hb) * gate).astype(BF16)
        else:
            out_ref[b] = hb


def _lru_pass(z3, h0, cw, cb, w, bvec, lam, rev, hf=None):
    batch, L, _ = z3.shape
    T = min(LRU_T, L)
    nchunks = L // T
    nb8 = L // 8
    cidx = (lambda s: nchunks - 1 - s) if rev else (lambda s: s)
    main = lambda col: pl.BlockSpec((batch, T, LRU_WIDTH), lambda s: (0, cidx(s), col))
    prev_map = lambda s: (0, jnp.maximum(cidx(s) * (T // 8) - 1, 0), ZX_COL512)
    next_map = lambda s: (0, jnp.minimum((cidx(s) + 1) * (T // 8), nb8 - 1), ZX_COL512)
    const2 = lambda shape: pl.BlockSpec(shape, lambda s: (0, 0))
    in_specs = [main(ZX_COL512),
                pl.BlockSpec((batch, 8, LRU_WIDTH), prev_map),
                pl.BlockSpec((batch, 8, LRU_WIDTH), next_map),
                const2((batch, LRU_WIDTH)),
                const2((4, LRU_WIDTH)), const2((1, LRU_WIDTH)),
                pl.BlockSpec((2, LRU_BLOCKS, LRU_BLOCK_W, LRU_BLOCK_W), lambda s: (0, 0, 0, 0)),
                const2((2, LRU_WIDTH)), const2((1, LRU_WIDTH))]
    args = [z3, z3, z3, h0, cw, cb, w, bvec, lam]
    if rev:
        in_specs += [main(ZG_COL512), pl.BlockSpec((batch, T, LRU_WIDTH), lambda s: (0, cidx(s), 0))]
        args += [z3, hf]
    out_dtype = BF16 if rev else F32
    S = T + LRU_PAD
    return pl.pallas_call(
        functools.partial(_lru_kernel, T=T, nchunks=nchunks, rev=rev, batch=batch),
        grid=(nchunks,),
        in_specs=in_specs,
        out_specs=[pl.BlockSpec((batch, T, LRU_WIDTH), lambda s: (0, cidx(s), 0)),
                   const2((batch, LRU_WIDTH))],
        out_shape=[jax.ShapeDtypeStruct((batch, L, LRU_WIDTH), out_dtype),
                   jax.ShapeDtypeStruct((batch, LRU_WIDTH), F32)],
        scratch_shapes=[pltpu.VMEM((LRU_BLOCKS, batch * S, LANES), F32),
                        pltpu.VMEM((LRU_BLOCKS, batch * S, LANES), F32),
                        pltpu.VMEM((batch, LRU_WIDTH), F32)],
        compiler_params=_params(("arbitrary",)),
        name="lru_rev" if rev else "lru_fwd",
    )(*args)


def _out_proj_kernel(a_ref, m_ref, r_ref, x_ref, mod_ref, g_ref, w_ref, o_ref):
    mix = jnp.concatenate([a_ref[...], m_ref[...], r_ref[...]], axis=-1)
    y = jnp.dot(mix, w_ref[...], preferred_element_type=F32)
    o_ref[...] = _gated_norm_residual(x_ref[...], y, g_ref[...], mod_ref[0][2:3])


def _out_proj(attn, mlp, lru, x, mod, g, w, layer, rows_per_mod, tm):
    M = x.shape[0]
    if rows_per_mod is None:
        mod_map = lambda i: (8, 0, 0)
    else:
        blocks = rows_per_mod // tm
        mod_map = lambda i: (i // blocks, 0, 0)
    return pl.pallas_call(
        _out_proj_kernel,
        grid=(M // tm,),
        in_specs=[
            pl.BlockSpec((tm, ATTN_WIDTH), lambda i: (i, 0)),
            pl.BlockSpec((tm, MLP_WIDTH), lambda i: (i, 0)),
            pl.BlockSpec((tm, LRU_WIDTH), lambda i: (i, 0)),
            pl.BlockSpec((tm, D_MODEL), lambda i: (i, 0)),
            pl.BlockSpec((1, 6, D_MODEL), mod_map),
            pl.BlockSpec((1, D_MODEL), lambda i: (0, 0)),
            pl.BlockSpec((None, D_MODEL, D_MODEL), lambda i: (layer, 0, 0)),
        ],
        out_specs=pl.BlockSpec((tm, D_MODEL), lambda i: (i, 0)),
        out_shape=jax.ShapeDtypeStruct((M, D_MODEL), F32),
        compiler_params=_params(("arbitrary",)),
        name="out_proj",
    )(attn, mlp, lru, x, mod, g, w)


def _ffn_kernel(x_ref, mod_ref, g2_ref, g3_ref, wg_ref, wu_ref, wo_ref, o_ref, h_scr, acc_scr):
    j = pl.program_id(1)
    m = mod_ref[0]

    @pl.when(j == 0)
    def _():
        h = _norm_modulate(x_ref[...], g2_ref[...], m[3:4], m[4:5])
        h_scr[...] = h.astype(BF16)
        acc_scr[...] = jnp.zeros_like(acc_scr)

    h = h_scr[...]
    gate = jnp.dot(h, wg_ref[...], preferred_element_type=F32)
    up = jnp.dot(h, wu_ref[...], preferred_element_type=F32)
    half_gate = 0.5 * gate
    act = (half_gate * (1.0 + jnp.tanh(half_gate)) * up).astype(BF16)
    acc_scr[...] += jnp.dot(act, wo_ref[...], preferred_element_type=F32)

    @pl.when(j == pl.num_programs(1) - 1)
    def _():
        o_ref[...] = _gated_norm_residual(x_ref[...], acc_scr[...], g3_ref[...], m[5:6])


def _ffn(x, mod, g2, g3, w_in, w_out, layer, rows_per_mod, tm, tf):
    M = x.shape[0]
    nf = D_FF // tf
    if rows_per_mod is None:
        mod_map = lambda i, j: (8, 0, 0)
    else:
        blocks = rows_per_mod // tm
        mod_map = lambda i, j: (i // blocks, 0, 0)
    return pl.pallas_call(
        _ffn_kernel,
        grid=(M // tm, nf),
        in_specs=[
            pl.BlockSpec((tm, D_MODEL), lambda i, j: (i, 0)),
            pl.BlockSpec((1, 6, D_MODEL), mod_map),
            pl.BlockSpec((1, D_MODEL), lambda i, j: (0, 0)),
            pl.BlockSpec((1, D_MODEL), lambda i, j: (0, 0)),
            pl.BlockSpec((None, D_MODEL, tf), lambda i, j: (layer, 0, j)),
            pl.BlockSpec((None, D_MODEL, tf), lambda i, j: (layer, 0, j + nf)),
            pl.BlockSpec((None, tf, D_MODEL), lambda i, j: (layer, j, 0)),
        ],
        out_specs=pl.BlockSpec((tm, D_MODEL), lambda i, j: (i, 0)),
        out_shape=jax.ShapeDtypeStruct((M, D_MODEL), F32),
        scratch_shapes=[pltpu.VMEM((tm, D_MODEL), BF16), pltpu.VMEM((tm, D_MODEL), F32)],
        compiler_params=_params(("arbitrary", "arbitrary")),
        name="ffn",
    )(x, mod, g2, g3, w_in, w_in, w_out)


def _rope_tables(n_tokens):
    rows = n_tokens // GRID_W
    row_ids = jnp.repeat(jnp.arange(rows, dtype=F32), GRID_W)
    col_ids = jnp.tile(jnp.arange(GRID_W, dtype=F32), rows)
    inv_freq = ROPE_THETA ** (-jnp.arange(0, AXIS_DIM, 2, dtype=F32) / AXIS_DIM)
    ang_r = row_ids[:, None] * inv_freq
    ang_c = col_ids[:, None] * inv_freq
    ang = jnp.concatenate([ang_r, ang_r, ang_c, ang_c], axis=-1)
    sign = jnp.where((jnp.arange(HEAD_DIM) % AXIS_DIM) < (AXIS_DIM // 2), -1.0, 1.0).astype(F32)
    return jnp.cos(ang), jnp.sin(ang) * sign


def kernel(x, c, ctx, c_ctx, w_mod, b_mod, g_norm, w_in, g_qk, w_s, b_s, conv_w, conv_b,
           lru_w, lru_b, lru_lam, w_out, w_ffn_in, w_ffn_out):
    batch, seq, d = x.shape
    ctx_len = ctx.shape[1]
    depth = w_mod.shape[0]
    cos, sin_signed = _rope_tables(seq)

    c_all = jnp.concatenate([c, c_ctx[None, :], jnp.zeros((MOD_ROWS - batch - 1, d), F32)], axis=0)
    mod_all = _modulation(c_all, w_mod, b_mod).reshape(depth, MOD_ROWS, 6, d)

    xl = x.reshape(batch * seq, d)
    xc = ctx.reshape(batch * ctx_len, d)
    zeros_h = jnp.zeros((batch, LRU_WIDTH), F32)
    w_in_b = w_in.astype(BF16)
    w_out_b = w_out.astype(BF16)
    w_fi_b = w_ffn_in.astype(BF16)
    w_fo_b = w_ffn_out.astype(BF16)

    for l in range(depth):
        last = l == depth - 1
        mod = mod_all[l]
        g = g_norm[l].reshape(4, 1, d)
        ws = w_s[l].astype(BF16)
        bias = jnp.repeat(b_s[l].T, CHUNK, axis=1)
        lw = (0.5 * lru_w[l]).astype(BF16)
        cb = conv_b[l].reshape(1, LRU_WIDTH)

        zl = _in_proj(xl, mod, g[0], w_in_b, l, seq, ROW_TILE)
        zc = _in_proj(xc, mod, g[0], w_in_b, l, None, ROW_TILE)

        attn = _attention(zl, zc, g_qk[l], cos, sin_signed, batch, seq, ctx_len, True, ATTN_TQ)
        mlp = _chunk_mlp(zl, ws, bias, ROW_TILE)

        zc3 = zc.reshape(batch, ctx_len, IN_COLS)
        zl3 = zl.reshape(batch, seq, IN_COLS)
        lru_args = lambda dd: (conv_w[l], cb, lw[dd], lru_b[l, dd], lru_lam[l, dd].reshape(1, LRU_WIDTH))
        hcf, h0f = _lru_pass(zc3, zeros_h, *lru_args(0), rev=False)
        lru_c, h0r = _lru_pass(zc3, zeros_h, *lru_args(1), rev=True, hf=hcf)
        hlf, _ = _lru_pass(zl3, h0f, *lru_args(0), rev=False)
        lru_l, _ = _lru_pass(zl3, h0r, *lru_args(1), rev=True, hf=hlf)

        xl = _out_proj(attn, mlp, lru_l.reshape(batch * seq, LRU_WIDTH), xl, mod, g[1], w_out_b, l, seq,
                       ROW_TILE)
        xl = _ffn(xl, mod, g[2], g[3], w_fi_b, w_fo_b, l, seq, ROW_TILE, FFN_TF)

        if not last:
            attn_c = _attention(zc, zc, g_qk[l], cos, sin_signed, batch, ctx_len, ctx_len, False, ctx_len)
            mlp_c = _chunk_mlp(zc, ws, bias, ROW_TILE)
            xc = _out_proj(attn_c, mlp_c, lru_c.reshape(batch * ctx_len, LRU_WIDTH), xc, mod, g[1], w_out_b,
                           l, None, ROW_TILE)
            xc = _ffn(xc, mod, g[2], g[3], w_fi_b, w_fo_b, l, None, ROW_TILE, FFN_TF)

    return xl.reshape(batch, seq, d)
```

```python
import functools

import jax
import jax.numpy as jnp
from jax import lax
from jax.experimental import pallas as pl
from jax.experimental.pallas import tpu as pltpu

F32 = jnp.float32
BF16 = jnp.bfloat16

D_MODEL = 2048
GRID_W = 64
EPS = 1e-6
HEAD_DIM = 128
N_Q_HEADS = 8
N_KV_HEADS = 2
Q_PER_KV = 4
ATTN_WIDTH = 1024
KV_WIDTH = 256
ROPE_THETA = 10000.0
AXIS_DIM = 64
CHUNK = 128
MLP_WIDTH = 512
MLP_GROUPS = 4
LRU_WIDTH = 512
LRU_BLOCKS = 4
LRU_C = 8.0
IN_COLS = 3584
D_FF = 5632

K_COL128 = ATTN_WIDTH // 128
V_COL128 = (ATTN_WIDTH + KV_WIDTH) // 128
ZM_COL512 = (ATTN_WIDTH + 2 * KV_WIDTH) // 512
ZX_COL512 = ZM_COL512 + 2
ZG_COL512 = ZX_COL512 + 1

VMEM_LIMIT = 56 * 1024 * 1024
MOD_ROWS = 16
LANES = 128
MLP_GROUP_W = MLP_WIDTH // MLP_GROUPS
LRU_BLOCK_W = LRU_WIDTH // LRU_BLOCKS
ROW_TILE = 512
ATTN_TQ = 512
ATTN_KC = 512
LOG2_E = 1.4426950408889634
FFN_TF = 512
FFN_PRE_SLICES = 8
LRU_T = 256
LRU_PAD = 8


def _params(sem):
    return pltpu.CompilerParams(dimension_semantics=sem, vmem_limit_bytes=VMEM_LIMIT)


def _unit_rms(xf):
    return xf * lax.rsqrt(jnp.mean(xf * xf, axis=-1, keepdims=True) + EPS)


def _norm_modulate(xf, g, shift, scale):
    return _unit_rms(xf) * (g * (1.0 + scale)) + shift


def _gated_norm_residual(x, y, g, gate):
    return x + _unit_rms(y) * (g * gate)


def _mod_kernel(c_ref, w_ref, b_ref, o_ref):
    c = c_ref[...]
    a = (c * jax.nn.sigmoid(c)).astype(BF16)
    o_ref[0] = jnp.dot(a, w_ref[0].astype(BF16), preferred_element_type=F32) + b_ref[0]


def _modulation(c_all, w_mod, b_mod):
    depth, d, n = w_mod.shape
    tn = 1024
    return pl.pallas_call(
        _mod_kernel,
        grid=(depth, n // tn),
        in_specs=[
            pl.BlockSpec((MOD_ROWS, d), lambda l, j: (0, 0)),
            pl.BlockSpec((1, d, tn), lambda l, j: (l, 0, j)),
            pl.BlockSpec((1, 1, tn), lambda l, j: (l, 0, j)),
        ],
        out_specs=pl.BlockSpec((1, MOD_ROWS, tn), lambda l, j: (l, 0, j)),
        out_shape=jax.ShapeDtypeStruct((depth, MOD_ROWS, n), F32),
        compiler_params=_params(("arbitrary", "arbitrary")),
        name="modulation",
    )(c_all, w_mod, b_mod.reshape(depth, 1, n))


def _in_proj_kernel(x_ref, mod_ref, g_ref, w_ref, z_ref):
    m = mod_ref[0]
    h = _norm_modulate(x_ref[...], g_ref[...], m[0:1], m[1:2])
    z_ref[...] = jnp.dot(h.astype(BF16), w_ref[...], preferred_element_type=F32).astype(BF16)


def _in_proj(x, mod, g, w, layer, rows_per_mod, tm):
    M = x.shape[0]
    if rows_per_mod is None:
        mod_map = lambda i: (8, 0, 0)
    else:
        blocks = rows_per_mod // tm
        mod_map = lambda i: (i // blocks, 0, 0)
    return pl.pallas_call(
        _in_proj_kernel,
        grid=(M // tm,),
        in_specs=[
            pl.BlockSpec((tm, D_MODEL), lambda i: (i, 0)),
            pl.BlockSpec((1, 6, D_MODEL), mod_map),
            pl.BlockSpec((1, D_MODEL), lambda i: (0, 0)),
            pl.BlockSpec((None, D_MODEL, IN_COLS), lambda i: (layer, 0, 0)),
        ],
        out_specs=pl.BlockSpec((tm, IN_COLS), lambda i: (i, 0)),
        out_shape=jax.ShapeDtypeStruct((M, IN_COLS), BF16),
        compiler_params=_params(("arbitrary",)),
        name="in_proj",
    )(x, mod, g, w)


def _head_norm(xf, g):
    return xf * lax.rsqrt(jnp.mean(xf * xf, axis=-1, keepdims=True) + EPS) * g


def _rope(xf, cos, sin_signed):
    lane = lax.broadcasted_iota(jnp.int32, xf.shape, 1)
    first = (lane % AXIS_DIM) < (AXIS_DIM // 2)
    partner = jnp.where(first, pltpu.roll(xf, HEAD_DIM - AXIS_DIM // 2, 1), pltpu.roll(xf, AXIS_DIM // 2, 1))
    return xf * cos + partner * sin_signed


_NT = (((1,), (1,)), ((), ()))


def _attn_kernel(*refs, tq, ctx_len, has_lat):
    if has_lat:
        (q_ref, kc_ref, vc_ref, kl_ref, vl_ref, cos_ref, sin_ref, gqk_ref, o_ref, kn_scr, vt_scr) = refs
    else:
        (q_ref, kc_ref, vc_ref, gqk_ref, o_ref, kn_scr, vt_scr) = refs
    qi = pl.program_id(2)
    gq = gqk_ref[0:1, :]
    gk = gqk_ref[1:2, :]
    n_keys = kn_scr.shape[0]

    @pl.when(qi == 0)
    def _():
        kn_scr[0:ctx_len, :] = _head_norm(kc_ref[...].astype(F32), gk).astype(BF16)
        vt_scr[:, 0:ctx_len] = vc_ref[...].astype(F32).T.astype(BF16)
        if has_lat:
            k = _head_norm(kl_ref[...].astype(F32), gk)
            kn_scr[ctx_len:, :] = _rope(k, cos_ref[...], sin_ref[...]).astype(BF16)
            vt_scr[:, ctx_len:] = vl_ref[...].astype(F32).T.astype(BF16)

    scale = HEAD_DIM ** -0.5 * LOG2_E
    heads = []
    for h in range(Q_PER_KV):
        qh = _head_norm(q_ref[:, h * HEAD_DIM:(h + 1) * HEAD_DIM].astype(F32), gq)
        if has_lat:
            rows = pl.ds(pl.multiple_of(qi * tq, tq), tq)
            qh = _rope(qh, cos_ref[rows, :], sin_ref[rows, :])
        heads.append((qh * scale).astype(BF16))

    qs = jnp.concatenate(heads, axis=0)

    s = lax.dot_general(kn_scr[...], qs, _NT, preferred_element_type=F32)
    m = jnp.max(s, axis=0, keepdims=True)
    l = jnp.zeros((1, Q_PER_KV * tq), F32)
    acc = jnp.zeros((HEAD_DIM, Q_PER_KV * tq), F32)
    bounds = [0, ctx_len] + list(range(ctx_len + ATTN_KC, n_keys + 1, ATTN_KC))
    for lo, hi in zip(bounds[:-1], bounds[1:]):
        rows = slice(lo, hi)
        p = jnp.exp2(s[rows] - m)
        l = l + jnp.sum(p, axis=0, keepdims=True)
        acc = acc + jnp.dot(vt_scr[:, rows], p.astype(BF16), preferred_element_type=F32)
    o = (acc / l).T
    for h in range(Q_PER_KV):
        o_ref[:, h * HEAD_DIM:(h + 1) * HEAD_DIM] = o[h * tq:(h + 1) * tq].astype(BF16)


def _attention(zq, zc, gqk, cos, sin_signed, batch, q_len, ctx_len, has_lat, tq):
    nq = q_len // tq
    n_keys = ctx_len + (q_len if has_lat else 0)
    kv_c = lambda col: pl.BlockSpec((ctx_len, HEAD_DIM), lambda b, k, i: (b, col + k))
    kv_l = lambda col: pl.BlockSpec((q_len, HEAD_DIM), lambda b, k, i: (b, col + k))
    in_specs = [pl.BlockSpec((tq, Q_PER_KV * HEAD_DIM), lambda b, k, i: (b * nq + i, k)),
                kv_c(K_COL128), kv_c(V_COL128)]
    args = [zq, zc, zc]
    scratch = [pltpu.VMEM((n_keys, HEAD_DIM), BF16), pltpu.VMEM((HEAD_DIM, n_keys), BF16)]
    if has_lat:
        in_specs += [kv_l(K_COL128), kv_l(V_COL128),
                     pl.BlockSpec((q_len, HEAD_DIM), lambda b, k, i: (0, 0)),
                     pl.BlockSpec((q_len, HEAD_DIM), lambda b, k, i: (0, 0))]
        args += [zq, zq, cos, sin_signed]
    in_specs.append(pl.BlockSpec((2, HEAD_DIM), lambda b, k, i: (0, 0)))
    args.append(gqk)
    return pl.pallas_call(
        functools.partial(_attn_kernel, tq=tq, ctx_len=ctx_len, has_lat=has_lat),
        grid=(batch, N_KV_HEADS, nq),
        in_specs=in_specs,
        out_specs=pl.BlockSpec((tq, Q_PER_KV * HEAD_DIM), lambda b, k, i: (b * nq + i, k)),
        out_shape=jax.ShapeDtypeStruct((batch * q_len, ATTN_WIDTH), BF16),
        scratch_shapes=scratch,
        compiler_params=_params(("arbitrary", "arbitrary", "arbitrary")),
        name="attention_lat" if has_lat else "attention_ctx",
    )(*args)


def _cmlp_kernel(zu_ref, zv_ref, ws_ref, bs_ref, o_ref, *, tm):
    bias = bs_ref[...]
    for n in range(tm // CHUNK):
        rows = slice(n * CHUNK, (n + 1) * CHUNK)
        v = jax.nn.gelu(zv_ref[rows, :].astype(F32)).astype(BF16)
        parts = [jnp.dot(ws_ref[g], v[:, g * MLP_GROUP_W:(g + 1) * MLP_GROUP_W], preferred_element_type=F32)
                 for g in range(MLP_GROUPS)]
        s = jnp.concatenate(parts, axis=-1) + bias
        u = jax.nn.gelu(zu_ref[rows, :].astype(F32))
        o_ref[rows, :] = (u * s).astype(BF16)


def _chunk_mlp(z, ws, bias, tm):
    M = z.shape[0]
    return pl.pallas_call(
        functools.partial(_cmlp_kernel, tm=tm),
        grid=(M // tm,),
        in_specs=[
            pl.BlockSpec((tm, MLP_WIDTH), lambda i: (i, ZM_COL512)),
            pl.BlockSpec((tm, MLP_WIDTH), lambda i: (i, ZM_COL512 + 1)),
            pl.BlockSpec((MLP_GROUPS, CHUNK, CHUNK), lambda i: (0, 0, 0)),
            pl.BlockSpec((CHUNK, MLP_WIDTH), lambda i: (0, 0)),
        ],
        out_specs=pl.BlockSpec((tm, MLP_WIDTH), lambda i: (i, 0)),
        out_shape=jax.ShapeDtypeStruct((M, MLP_WIDTH), BF16),
        compiler_params=_params(("arbitrary",)),
        name="chunk_mlp",
    )(z, z, ws, bias)


def _lru_kernel(*refs, T, nchunks, rev, batch):
    if rev:
        (zx_ref, zp_ref, zn_ref, h0_ref, cw_ref, cb_ref, w_ref, b_ref, lam_ref, zg_ref, hf_ref,
         out_ref, hfin_ref, a_scr, bx_scr, hc_scr) = refs
    else:
        (zx_ref, zp_ref, zn_ref, h0_ref, cw_ref, cb_ref, w_ref, b_ref, lam_ref,
         out_ref, hfin_ref, a_scr, bx_scr, hc_scr) = refs
    S = T + LRU_PAD
    s = pl.program_id(0)
    c = (nchunks - 1 - s) if rev else s

    @pl.when(s == 0)
    def _():
        hc_scr[...] = h0_ref[...]

    cw = cw_ref[...]
    cb = cb_ref[...]
    lam = lam_ref[...]
    neg_lam = -lam
    softplus = jnp.maximum(neg_lam, 0.0) + jnp.log1p(jnp.exp(-jnp.abs(neg_lam)))
    half_coef2 = (-0.5 * LRU_C * LOG2_E) * softplus
    half_ba = 0.5 * b_ref[0:1, :]
    half_bi = 0.5 * b_ref[1:2, :]
    at_start = c == 0
    at_end = c == nchunks - 1
    TE = T + 16

    for b in range(batch):
        x = zx_ref[b].astype(F32)
        prev = jnp.where(at_start, 0.0, zp_ref[b].astype(F32))
        nxt = jnp.where(at_end, 0.0, zn_ref[b].astype(F32))
        xe = jnp.concatenate([prev, x, nxt], axis=0)
        xm2 = pltpu.roll(xe, 2, 0)[8:T + 8]
        xm1 = pltpu.roll(xe, 1, 0)[8:T + 8]
        xp1 = pltpu.roll(xe, TE - 1, 0)[8:T + 8]
        xr = cb + xm2 * cw[0:1] + xm1 * cw[1:2] + x * cw[2:3] + xp1 * cw[3:4]
        xb = xr.astype(BF16)
        ga = jnp.concatenate(
            [jnp.dot(xb[:, k * LANES:(k + 1) * LANES], w_ref[0, k], preferred_element_type=F32)
             for k in range(LRU_BLOCKS)], axis=-1)
        gi = jnp.concatenate(
            [jnp.dot(xb[:, k * LANES:(k + 1) * LANES], w_ref[1, k], preferred_element_type=F32)
             for k in range(LRU_BLOCKS)], axis=-1)
        a = jnp.exp2(half_coef2 * (1.0 + jnp.tanh(ga + half_ba)))
        i = 0.5 * (1.0 + jnp.tanh(gi + half_bi))
        y = 1.0 - a * a
        bx = jnp.where(y > 0.0, y * lax.rsqrt(y), 0.0) * (i * xr)
        for k in range(LRU_BLOCKS):
            a_scr[k, b * S:b * S + T, :] = a[:, k * LANES:(k + 1) * LANES]
            bx_scr[k, b * S:b * S + T, :] = bx[:, k * LANES:(k + 1) * LANES]

    def step(n, hs):
        tt = (T - 1 - n) if rev else n
        rows = pl.ds(tt, batch, stride=S)
        out = []
        for k in range(LRU_BLOCKS):
            h = a_scr[k, rows, :] * hs[k] + bx_scr[k, rows, :]
            bx_scr[k, rows, :] = h
            out.append(h)
        return tuple(out)

    h_init = tuple(hc_scr[:, k * LANES:(k + 1) * LANES] for k in range(LRU_BLOCKS))
    h_last = jnp.concatenate(lax.fori_loop(0, T, step, h_init, unroll=4), axis=-1)
    hc_scr[...] = h_last
    hfin_ref[...] = h_last

    for b in range(batch):
        hb = jnp.concatenate([bx_scr[k, b * S:b * S + T, :] for k in range(LRU_BLOCKS)], axis=-1)
        if rev:
            gate = jax.nn.gelu(zg_ref[b].astype(F32))
            out_ref[b] = ((hf_ref[b] + hb) * gate).astype(BF16)
        else:
            out_ref[b] = hb


def _lru_pass(z3, h0, cw, cb, w, bvec, lam, rev, hf=None):
    batch, L, _ = z3.shape
    T = min(LRU_T, L)
    nchunks = L // T
    nb8 = L // 8
    cidx = (lambda s: nchunks - 1 - s) if rev else (lambda s: s)
    main = lambda col: pl.BlockSpec((batch, T, LRU_WIDTH), lambda s: (0, cidx(s), col))
    prev_map = lambda s: (0, jnp.maximum(cidx(s) * (T // 8) - 1, 0), ZX_COL512)
    next_map = lambda s: (0, jnp.minimum((cidx(s) + 1) * (T // 8), nb8 - 1), ZX_COL512)
    const2 = lambda shape: pl.BlockSpec(shape, lambda s: (0, 0))
    in_specs = [main(ZX_COL512),
                pl.BlockSpec((batch, 8, LRU_WIDTH), prev_map),
                pl.BlockSpec((batch, 8, LRU_WIDTH), next_map),
                const2((batch, LRU_WIDTH)),
                const2((4, LRU_WIDTH)), const2((1, LRU_WIDTH)),
                pl.BlockSpec((2, LRU_BLOCKS, LRU_BLOCK_W, LRU_BLOCK_W), lambda s: (0, 0, 0, 0)),
                const2((2, LRU_WIDTH)), const2((1, LRU_WIDTH))]
    args = [z3, z3, z3, h0, cw, cb, w, bvec, lam]
    if rev:
        in_specs += [main(ZG_COL512), pl.BlockSpec((batch, T, LRU_WIDTH), lambda s: (0, cidx(s), 0))]
        args += [z3, hf]
    out_dtype = BF16 if rev else F32
    S = T + LRU_PAD
    return pl.pallas_call(
        functools.partial(_lru_kernel, T=T, nchunks=nchunks, rev=rev, batch=batch),
        grid=(nchunks,),
        in_specs=in_specs,
        out_specs=[pl.BlockSpec((batch, T, LRU_WIDTH), lambda s: (0, cidx(s), 0)),
                   const2((batch, LRU_WIDTH))],
        out_shape=[jax.ShapeDtypeStruct((batch, L, LRU_WIDTH), out_dtype),
                   jax.ShapeDtypeStruct((batch, LRU_WIDTH), F32)],
        scratch_shapes=[pltpu.VMEM((LRU_BLOCKS, batch * S, LANES), F32),
                        pltpu.VMEM((LRU_BLOCKS, batch * S, LANES), F32),
                        pltpu.VMEM((batch, LRU_WIDTH), F32)],
        compiler_params=_params(("arbitrary",)),
        name="lru_rev" if rev else "lru_fwd",
    )(*args)


def _out_proj_kernel(a_ref, m_ref, r_ref, x_ref, mod_ref, g_ref, w_ref, o_ref):
    mix = jnp.concatenate([a_ref[...], m_ref[...], r_ref[...]], axis=-1)
    y = jnp.dot(mix, w_ref[...], preferred_element_type=F32)
    o_ref[...] = _gated_norm_residual(x_ref[...], y, g_ref[...], mod_ref[0][2:3])


def _out_proj(attn, mlp, lru, x, mod, g, w, layer, rows_per_mod, tm):
    M = x.shape[0]
    if rows_per_mod is None:
        mod_map = lambda i: (8, 0, 0)
    else:
        blocks = rows_per_mod // tm
        mod_map = lambda i: (i // blocks, 0, 0)
    return pl.pallas_call(
        _out_proj_kernel,
        grid=(M // tm,),
        in_specs=[
            pl.BlockSpec((tm, ATTN_WIDTH), lambda i: (i, 0)),
            pl.BlockSpec((tm, MLP_WIDTH), lambda i: (i, 0)),
            pl.BlockSpec((tm, LRU_WIDTH), lambda i: (i, 0)),
            pl.BlockSpec((tm, D_MODEL), lambda i: (i, 0)),
            pl.BlockSpec((1, 6, D_MODEL), mod_map),
            pl.BlockSpec((1, D_MODEL), lambda i: (0, 0)),
            pl.BlockSpec((None, D_MODEL, D_MODEL), lambda i: (layer, 0, 0)),
        ],
        out_specs=pl.BlockSpec((tm, D_MODEL), lambda i: (i, 0)),
        out_shape=jax.ShapeDtypeStruct((M, D_MODEL), F32),
        compiler_params=_params(("arbitrary",)),
        name="out_proj",
    )(attn, mlp, lru, x, mod, g, w)


def _ffn_kernel(x_ref, xn_ref, mod_ref, modn_ref, g2_ref, g3_ref, wg_ref, wu_ref, wo_ref, o_ref, h_scr, acc_scr):
    i = pl.program_id(0)
    j = pl.program_id(1)
    m = mod_ref[0]
    slot = i % 2

    @pl.when((i == 0) & (j == 0))
    def _():
        h_scr[0] = _norm_modulate(x_ref[...], g2_ref[...], m[3:4], m[4:5]).astype(BF16)
        acc_scr[...] = jnp.zeros_like(acc_scr)

    rows_per = x_ref.shape[0] // FFN_PRE_SLICES
    r = pl.ds(pl.multiple_of(jnp.minimum(j, FFN_PRE_SLICES - 1) * rows_per, rows_per), rows_per)
    mn = modn_ref[0]

    h = h_scr[slot]
    gate = jnp.dot(h, wg_ref[...], preferred_element_type=F32)
    up = jnp.dot(h, wu_ref[...], preferred_element_type=F32)
    half_gate = 0.5 * gate
    act = (half_gate * (1.0 + jnp.tanh(half_gate)) * up).astype(BF16)
    down = jnp.dot(act, wo_ref[...], preferred_element_type=F32)
    h_scr[1 - slot, r, :] = _norm_modulate(xn_ref[r, :], g2_ref[...], mn[3:4], mn[4:5]).astype(BF16)
    acc_scr[...] = down + jnp.where(j == 0, 0.0, acc_scr[...])

    @pl.when(j == pl.num_programs(1) - 1)
    def _():
        o_ref[...] = _gated_norm_residual(x_ref[...], acc_scr[...], g3_ref[...], m[5:6])


def _ffn(x, mod, g2, g3, w_in, w_out, layer, rows_per_mod, tm, tf):
    M = x.shape[0]
    nf = D_FF // tf
    nblk = M // tm
    assert nf >= FFN_PRE_SLICES and tm % (16 * FFN_PRE_SLICES) == 0
    nxt = lambda i: jnp.minimum(i + 1, nblk - 1)
    if rows_per_mod is None:
        mod_of = lambda i: 8
    else:
        blocks = rows_per_mod // tm
        mod_of = lambda i: i // blocks
    return pl.pallas_call(
        _ffn_kernel,
        grid=(nblk, nf),
        in_specs=[
            pl.BlockSpec((tm, D_MODEL), lambda i, j: (i, 0)),
            pl.BlockSpec((tm, D_MODEL), lambda i, j: (nxt(i), 0)),
            pl.BlockSpec((1, 6, D_MODEL), lambda i, j: (mod_of(i), 0, 0)),
            pl.BlockSpec((1, 6, D_MODEL), lambda i, j: (mod_of(nxt(i)), 0, 0)),
            pl.BlockSpec((1, D_MODEL), lambda i, j: (0, 0)),
            pl.BlockSpec((1, D_MODEL), lambda i, j: (0, 0)),
            pl.BlockSpec((None, D_MODEL, tf), lambda i, j: (layer, 0, j)),
            pl.BlockSpec((None, D_MODEL, tf), lambda i, j: (layer, 0, j + nf)),
            pl.BlockSpec((None, tf, D_MODEL), lambda i, j: (layer, j, 0)),
        ],
        out_specs=pl.BlockSpec((tm, D_MODEL), lambda i, j: (i, 0)),
        out_shape=jax.ShapeDtypeStruct((M, D_MODEL), F32),
        scratch_shapes=[pltpu.VMEM((2, tm, D_MODEL), BF16), pltpu.VMEM((tm, D_MODEL), F32)],
        compiler_params=_params(("arbitrary", "arbitrary")),
        name="ffn",
    )(x, x, mod, mod, g2, g3, w_in, w_in, w_out)


def _rope_tables(n_tokens):
    rows = n_tokens // GRID_W
    row_ids = jnp.repeat(jnp.arange(rows, dtype=F32), GRID_W)
    col_ids = jnp.tile(jnp.arange(GRID_W, dtype=F32), rows)
    inv_freq = ROPE_THETA ** (-jnp.arange(0, AXIS_DIM, 2, dtype=F32) / AXIS_DIM)
    ang_r = row_ids[:, None] * inv_freq
    ang_c = col_ids[:, None] * inv_freq
    ang = jnp.concatenate([ang_r, ang_r, ang_c, ang_c], axis=-1)
    sign = jnp.where((jnp.arange(HEAD_DIM) % AXIS_DIM) < (AXIS_DIM // 2), -1.0, 1.0).astype(F32)
    return jnp.cos(ang), jnp.sin(ang) * sign


def kernel(x, c, ctx, c_ctx, w_mod, b_mod, g_norm, w_in, g_qk, w_s, b_s, conv_w, conv_b,
           lru_w, lru_b, lru_lam, w_out, w_ffn_in, w_ffn_out):
    batch, seq, d = x.shape
    ctx_len = ctx.shape[1]
    depth = w_mod.shape[0]
    cos, sin_signed = _rope_tables(seq)

    c_all = jnp.concatenate([c, c_ctx[None, :], jnp.zeros((MOD_ROWS - batch - 1, d), F32)], axis=0)
    mod_all = _modulation(c_all, w_mod, b_mod).reshape(depth, MOD_ROWS, 6, d)

    xl = x.reshape(batch * seq, d)
    xc = ctx.reshape(batch * ctx_len, d)
    zeros_h = jnp.zeros((batch, LRU_WIDTH), F32)
    w_in_b = w_in.astype(BF16)
    w_out_b = w_out.astype(BF16)
    w_fi_b = w_ffn_in.astype(BF16)
    w_fo_b = w_ffn_out.astype(BF16)

    for l in range(depth):
        last = l == depth - 1
        mod = mod_all[l]
        g = g_norm[l].reshape(4, 1, d)
        ws = w_s[l].astype(BF16)
        bias = jnp.repeat(b_s[l].T, CHUNK, axis=1)
        lw = (0.5 * lru_w[l]).astype(BF16)
        cb = conv_b[l].reshape(1, LRU_WIDTH)

        zl = _in_proj(xl, mod, g[0], w_in_b, l, seq, ROW_TILE)
        zc = _in_proj(xc, mod, g[0], w_in_b, l, None, ROW_TILE)

        attn = _attention(zl, zc, g_qk[l], cos, sin_signed, batch, seq, ctx_len, True, ATTN_TQ)
        mlp = _chunk_mlp(zl, ws, bias, ROW_TILE)

        zc3 = zc.reshape(batch, ctx_len, IN_COLS)
        zl3 = zl.reshape(batch, seq, IN_COLS)
        lru_args = lambda dd: (conv_w[l], cb, lw[dd], lru_b[l, dd], lru_lam[l, dd].reshape(1, LRU_WIDTH))
        hcf, h0f = _lru_pass(zc3, zeros_h, *lru_args(0), rev=False)
        lru_c, h0r = _lru_pass(zc3, zeros_h, *lru_args(1), rev=True, hf=hcf)
        hlf, _ = _lru_pass(zl3, h0f, *lru_args(0), rev=False)
        lru_l, _ = _lru_pass(zl3, h0r, *lru_args(1), rev=True, hf=hlf)

        xl = _out_proj(attn, mlp, lru_l.reshape(batch * seq, LRU_WIDTH), xl, mod, g[1], w_out_b, l, seq,
                       ROW_TILE)
        xl = _ffn(xl, mod, g[2], g[3], w_fi_b, w_fo_b, l, seq, ROW_TILE, FFN_TF)

        if not last:
            attn_c = _attention(zc, zc, g_qk[l], cos, sin_signed, batch, ctx_len, ctx_len, False, ctx_len)
            mlp_c = _chunk_mlp(zc, ws, bias, ROW_TILE)
            xc = _out_proj(attn_c, mlp_c, lru_c.reshape(batch * ctx_len, LRU_WIDTH), xc, mod, g[1], w_out_b,
                           l, None, ROW_TILE)
            xc = _ffn(xc, mod, g[2], g[3], w_fi_b, w_fo_b, l, None, ROW_TILE, FFN_TF)

    return xl.reshape(batch, seq, d)
```

```python
import functools

import jax
import jax.numpy as jnp
from jax import lax
from jax.experimental import pallas as pl
from jax.experimental.pallas import tpu as pltpu

F32 = jnp.float32
BF16 = jnp.bfloat16

D_MODEL = 2048
GRID_W = 64
EPS = 1e-6
HEAD_DIM = 128
N_Q_HEADS = 8
N_KV_HEADS = 2
Q_PER_KV = 4
ATTN_WIDTH = 1024
KV_WIDTH = 256
ROPE_THETA = 10000.0
AXIS_DIM = 64
CHUNK = 128
MLP_WIDTH = 512
MLP_GROUPS = 4
LRU_WIDTH = 512
LRU_BLOCKS = 4
LRU_C = 8.0
IN_COLS = 3584
D_FF = 5632

K_COL128 = ATTN_WIDTH // 128
V_COL128 = (ATTN_WIDTH + KV_WIDTH) // 128
ZM_COL512 = (ATTN_WIDTH + 2 * KV_WIDTH) // 512
ZX_COL512 = ZM_COL512 + 2
ZG_COL512 = ZX_COL512 + 1

VMEM_LIMIT = 56 * 1024 * 1024
MOD_ROWS = 16
LANES = 128
MLP_GROUP_W = MLP_WIDTH // MLP_GROUPS
LRU_BLOCK_W = LRU_WIDTH // LRU_BLOCKS
ROW_TILE = 512
ATTN_TQ = 512
ATTN_KC = 512
LOG2_E = 1.4426950408889634
FFN_TF = 512
LRU_T = 256
LRU_PAD = 8


def _params(sem):
    return pltpu.CompilerParams(dimension_semantics=sem, vmem_limit_bytes=VMEM_LIMIT)


def _unit_rms(xf):
    return xf * lax.rsqrt(jnp.mean(xf * xf, axis=-1, keepdims=True) + EPS)


def _norm_modulate(xf, g, shift, scale):
    return _unit_rms(xf) * (g * (1.0 + scale)) + shift


def _gated_norm_residual(x, y, g, gate):
    return x + _unit_rms(y) * (g * gate)


def _mod_kernel(c_ref, w_ref, b_ref, o_ref):
    c = c_ref[...]
    a = (c * jax.nn.sigmoid(c)).astype(BF16)
    o_ref[0] = jnp.dot(a, w_ref[0].astype(BF16), preferred_element_type=F32) + b_ref[0]


def _modulation(c_all, w_mod, b_mod):
    depth, d, n = w_mod.shape
    tn = 1024
    return pl.pallas_call(
        _mod_kernel,
        grid=(depth, n // tn),
        in_specs=[
            pl.BlockSpec((MOD_ROWS, d), lambda l, j: (0, 0)),
            pl.BlockSpec((1, d, tn), lambda l, j: (l, 0, j)),
            pl.BlockSpec((1, 1, tn), lambda l, j: (l, 0, j)),
        ],
        out_specs=pl.BlockSpec((1, MOD_ROWS, tn), lambda l, j: (l, 0, j)),
        out_shape=jax.ShapeDtypeStruct((depth, MOD_ROWS, n), F32),
        compiler_params=_params(("arbitrary", "arbitrary")),
        name="modulation",
    )(c_all, w_mod, b_mod.reshape(depth, 1, n))


def _in_proj_kernel(x_ref, mod_ref, g_ref, w_ref, z_ref):
    m = mod_ref[0]
    h = _norm_modulate(x_ref[...], g_ref[...], m[0:1], m[1:2])
    z_ref[...] = jnp.dot(h.astype(BF16), w_ref[...], preferred_element_type=F32).astype(BF16)


def _in_proj(x, mod, g, w, layer, rows_per_mod, tm):
    M = x.shape[0]
    if rows_per_mod is None:
        mod_map = lambda i: (8, 0, 0)
    else:
        blocks = rows_per_mod // tm
        mod_map = lambda i: (i // blocks, 0, 0)
    return pl.pallas_call(
        _in_proj_kernel,
        grid=(M // tm,),
        in_specs=[
            pl.BlockSpec((tm, D_MODEL), lambda i: (i, 0)),
            pl.BlockSpec((1, 6, D_MODEL), mod_map),
            pl.BlockSpec((1, D_MODEL), lambda i: (0, 0)),
            pl.BlockSpec((None, D_MODEL, IN_COLS), lambda i: (layer, 0, 0), pipeline_mode=pl.Buffered(1)),
        ],
        out_specs=pl.BlockSpec((tm, IN_COLS), lambda i: (i, 0)),
        out_shape=jax.ShapeDtypeStruct((M, IN_COLS), BF16),
        compiler_params=_params(("arbitrary",)),
        name="in_proj",
    )(x, mod, g, w)


def _head_norm(xf, g):
    return xf * lax.rsqrt(jnp.mean(xf * xf, axis=-1, keepdims=True) + EPS) * g


def _rope(xf, cos, sin_signed):
    lane = lax.broadcasted_iota(jnp.int32, xf.shape, 1)
    first = (lane % AXIS_DIM) < (AXIS_DIM // 2)
    partner = jnp.where(first, pltpu.roll(xf, HEAD_DIM - AXIS_DIM // 2, 1), pltpu.roll(xf, AXIS_DIM // 2, 1))
    return xf * cos + partner * sin_signed


_NT = (((1,), (1,)), ((), ()))


def _attn_kernel(*refs, tq, ctx_len, has_lat):
    if has_lat:
        (q_ref, kc_ref, vc_ref, kl_ref, vl_ref, cos_ref, sin_ref, gqk_ref, o_ref, kn_scr, vt_scr) = refs
    else:
        (q_ref, kc_ref, vc_ref, gqk_ref, o_ref, kn_scr, vt_scr) = refs
    qi = pl.program_id(2)
    gq = gqk_ref[0:1, :]
    gk = gqk_ref[1:2, :]
    n_keys = kn_scr.shape[0]

    @pl.when(qi == 0)
    def _():
        kn_scr[0:ctx_len, :] = _head_norm(kc_ref[...].astype(F32), gk).astype(BF16)
        vt_scr[:, 0:ctx_len] = vc_ref[...].astype(F32).T.astype(BF16)
        if has_lat:
            k = _head_norm(kl_ref[...].astype(F32), gk)
            kn_scr[ctx_len:, :] = _rope(k, cos_ref[...], sin_ref[...]).astype(BF16)
            vt_scr[:, ctx_len:] = vl_ref[...].astype(F32).T.astype(BF16)

    scale = HEAD_DIM ** -0.5 * LOG2_E
    heads = []
    for h in range(Q_PER_KV):
        qh = _head_norm(q_ref[:, h * HEAD_DIM:(h + 1) * HEAD_DIM].astype(F32), gq)
        if has_lat:
            rows = pl.ds(pl.multiple_of(qi * tq, tq), tq)
            qh = _rope(qh, cos_ref[rows, :], sin_ref[rows, :])
        heads.append((qh * scale).astype(BF16))

    qs = jnp.concatenate(heads, axis=0)

    s = lax.dot_general(kn_scr[...], qs, _NT, preferred_element_type=F32)
    m = jnp.max(s, axis=0, keepdims=True)
    l = jnp.zeros((1, Q_PER_KV * tq), F32)
    acc = jnp.zeros((HEAD_DIM, Q_PER_KV * tq), F32)
    bounds = [0, ctx_len] + list(range(ctx_len + ATTN_KC, n_keys + 1, ATTN_KC))
    for lo, hi in zip(bounds[:-1], bounds[1:]):
        rows = slice(lo, hi)
        p = jnp.exp2(s[rows] - m)
        l = l + jnp.sum(p, axis=0, keepdims=True)
        acc = acc + jnp.dot(vt_scr[:, rows], p.astype(BF16), preferred_element_type=F32)
    o = (acc / l).T
    for h in range(Q_PER_KV):
        o_ref[:, h * HEAD_DIM:(h + 1) * HEAD_DIM] = o[h * tq:(h + 1) * tq].astype(BF16)


def _attention(zq, zc, gqk, cos, sin_signed, batch, q_len, ctx_len, has_lat, tq):
    nq = q_len // tq
    n_keys = ctx_len + (q_len if has_lat else 0)
    kv_c = lambda col: pl.BlockSpec((ctx_len, HEAD_DIM), lambda b, k, i: (b, col + k))
    kv_l = lambda col: pl.BlockSpec((q_len, HEAD_DIM), lambda b, k, i: (b, col + k))
    in_specs = [pl.BlockSpec((tq, Q_PER_KV * HEAD_DIM), lambda b, k, i: (b * nq + i, k)),
                kv_c(K_COL128), kv_c(V_COL128)]
    args = [zq, zc, zc]
    scratch = [pltpu.VMEM((n_keys, HEAD_DIM), BF16), pltpu.VMEM((HEAD_DIM, n_keys), BF16)]
    if has_lat:
        in_specs += [kv_l(K_COL128), kv_l(V_COL128),
                     pl.BlockSpec((q_len, HEAD_DIM), lambda b, k, i: (0, 0)),
                     pl.BlockSpec((q_len, HEAD_DIM), lambda b, k, i: (0, 0))]
        args += [zq, zq, cos, sin_signed]
    in_specs.append(pl.BlockSpec((2, HEAD_DIM), lambda b, k, i: (0, 0)))
    args.append(gqk)
    return pl.pallas_call(
        functools.partial(_attn_kernel, tq=tq, ctx_len=ctx_len, has_lat=has_lat),
        grid=(batch, N_KV_HEADS, nq),
        in_specs=in_specs,
        out_specs=pl.BlockSpec((tq, Q_PER_KV * HEAD_DIM), lambda b, k, i: (b * nq + i, k)),
        out_shape=jax.ShapeDtypeStruct((batch * q_len, ATTN_WIDTH), BF16),
        scratch_shapes=scratch,
        compiler_params=_params(("arbitrary", "arbitrary", "arbitrary")),
        name="attention_lat" if has_lat else "attention_ctx",
    )(*args)


def _cmlp_kernel(zu_ref, zv_ref, ws_ref, bs_ref, o_ref, *, tm):
    bias = bs_ref[...]
    for n in range(tm // CHUNK):
        rows = slice(n * CHUNK, (n + 1) * CHUNK)
        v = jax.nn.gelu(zv_ref[rows, :].astype(F32)).astype(BF16)
        parts = [jnp.dot(ws_ref[g], v[:, g * MLP_GROUP_W:(g + 1) * MLP_GROUP_W], preferred_element_type=F32)
                 for g in range(MLP_GROUPS)]
        s = jnp.concatenate(parts, axis=-1) + bias
        u = jax.nn.gelu(zu_ref[rows, :].astype(F32))
        o_ref[rows, :] = (u * s).astype(BF16)


def _chunk_mlp(z, ws, bias, tm):
    M = z.shape[0]
    return pl.pallas_call(
        functools.partial(_cmlp_kernel, tm=tm),
        grid=(M // tm,),
        in_specs=[
            pl.BlockSpec((tm, MLP_WIDTH), lambda i: (i, ZM_COL512)),
            pl.BlockSpec((tm, MLP_WIDTH), lambda i: (i, ZM_COL512 + 1)),
            pl.BlockSpec((MLP_GROUPS, CHUNK, CHUNK), lambda i: (0, 0, 0)),
            pl.BlockSpec((CHUNK, MLP_WIDTH), lambda i: (0, 0)),
        ],
        out_specs=pl.BlockSpec((tm, MLP_WIDTH), lambda i: (i, 0)),
        out_shape=jax.ShapeDtypeStruct((M, MLP_WIDTH), BF16),
        compiler_params=_params(("arbitrary",)),
        name="chunk_mlp",
    )(z, z, ws, bias)


def _lru_kernel(*refs, T, nchunks, rev, batch):
    if rev:
        (zx_ref, zp_ref, zn_ref, h0_ref, cw_ref, cb_ref, w_ref, b_ref, lam_ref, zg_ref, hf_ref,
         out_ref, hfin_ref, a_scr, bx_scr, hc_scr) = refs
    else:
        (zx_ref, zp_ref, zn_ref, h0_ref, cw_ref, cb_ref, w_ref, b_ref, lam_ref,
         out_ref, hfin_ref, a_scr, bx_scr, hc_scr) = refs
    S = T + LRU_PAD
    s = pl.program_id(0)
    c = (nchunks - 1 - s) if rev else s

    @pl.when(s == 0)
    def _():
        hc_scr[...] = h0_ref[...]

    cw = cw_ref[...]
    cb = cb_ref[...]
    lam = lam_ref[...]
    neg_lam = -lam
    softplus = jnp.maximum(neg_lam, 0.0) + jnp.log1p(jnp.exp(-jnp.abs(neg_lam)))
    half_coef2 = (-0.5 * LRU_C * LOG2_E) * softplus
    half_ba = 0.5 * b_ref[0:1, :]
    half_bi = 0.5 * b_ref[1:2, :]
    at_start = c == 0
    at_end = c == nchunks - 1
    TE = T + 16

    for b in range(batch):
        x = zx_ref[b].astype(F32)
        prev = jnp.where(at_start, 0.0, zp_ref[b].astype(F32))
        nxt = jnp.where(at_end, 0.0, zn_ref[b].astype(F32))
        xe = jnp.concatenate([prev, x, nxt], axis=0)
        xm2 = pltpu.roll(xe, 2, 0)[8:T + 8]
        xm1 = pltpu.roll(xe, 1, 0)[8:T + 8]
        xp1 = pltpu.roll(xe, TE - 1, 0)[8:T + 8]
        xr = cb + xm2 * cw[0:1] + xm1 * cw[1:2] + x * cw[2:3] + xp1 * cw[3:4]
        xb = xr.astype(BF16)
        ga = jnp.concatenate(
            [jnp.dot(xb[:, k * LANES:(k + 1) * LANES], w_ref[0, k], preferred_element_type=F32)
             for k in range(LRU_BLOCKS)], axis=-1)
        gi = jnp.concatenate(
            [jnp.dot(xb[:, k * LANES:(k + 1) * LANES], w_ref[1, k], preferred_element_type=F32)
             for k in range(LRU_BLOCKS)], axis=-1)
        a = jnp.exp2(half_coef2 * (1.0 + jnp.tanh(ga + half_ba)))
        i = 0.5 * (1.0 + jnp.tanh(gi + half_bi))
        y = 1.0 - a * a
        bx = jnp.where(y > 0.0, y * lax.rsqrt(y), 0.0) * (i * xr)
        for k in range(LRU_BLOCKS):
            a_scr[k, b * S:b * S + T, :] = a[:, k * LANES:(k + 1) * LANES]
            bx_scr[k, b * S:b * S + T, :] = bx[:, k * LANES:(k + 1) * LANES]

    def step(n, hs):
        tt = (T - 1 - n) if rev else n
        rows = pl.ds(tt, batch, stride=S)
        out = []
        for k in range(LRU_BLOCKS):
            h = a_scr[k, rows, :] * hs[k] + bx_scr[k, rows, :]
            bx_scr[k, rows, :] = h
            out.append(h)
        return tuple(out)

    h_init = tuple(hc_scr[:, k * LANES:(k + 1) * LANES] for k in range(LRU_BLOCKS))
    h_last = jnp.concatenate(lax.fori_loop(0, T, step, h_init, unroll=4), axis=-1)
    hc_scr[...] = h_last
    hfin_ref[...] = h_last

    for b in range(batch):
        hb = jnp.concatenate([bx_scr[k, b * S:b * S + T, :] for k in range(LRU_BLOCKS)], axis=-1)
        if rev:
            gate = jax.nn.gelu(zg_ref[b].astype(F32))
            out_ref[b] = ((hf_ref[b] + hb) * gate).astype(BF16)
        else:
            out_ref[b] = hb


def _lru_pass(z3, h0, cw, cb, w, bvec, lam, rev, hf=None):
    batch, L, _ = z3.shape
    T = min(LRU_T, L)
    nchunks = L // T
    nb8 = L // 8
    cidx = (lambda s: nchunks - 1 - s) if rev else (lambda s: s)
    main = lambda col: pl.BlockSpec((batch, T, LRU_WIDTH), lambda s: (0, cidx(s), col))
    prev_map = lambda s: (0, jnp.maximum(cidx(s) * (T // 8) - 1, 0), ZX_COL512)
    next_map = lambda s: (0, jnp.minimum((cidx(s) + 1) * (T // 8), nb8 - 1), ZX_COL512)
    const2 = lambda shape: pl.BlockSpec(shape, lambda s: (0, 0))
    in_specs = [main(ZX_COL512),
                pl.BlockSpec((batch, 8, LRU_WIDTH), prev_map),
                pl.BlockSpec((batch, 8, LRU_WIDTH), next_map),
                const2((batch, LRU_WIDTH)),
                const2((4, LRU_WIDTH)), const2((1, LRU_WIDTH)),
                pl.BlockSpec((2, LRU_BLOCKS, LRU_BLOCK_W, LRU_BLOCK_W), lambda s: (0, 0, 0, 0)),
                const2((2, LRU_WIDTH)), const2((1, LRU_WIDTH))]
    args = [z3, z3, z3, h0, cw, cb, w, bvec, lam]
    if rev:
        in_specs += [main(ZG_COL512), pl.BlockSpec((batch, T, LRU_WIDTH), lambda s: (0, cidx(s), 0))]
        args += [z3, hf]
    out_dtype = BF16 if rev else F32
    S = T + LRU_PAD
    return pl.pallas_call(
        functools.partial(_lru_kernel, T=T, nchunks=nchunks, rev=rev, batch=batch),
        grid=(nchunks,),
        in_specs=in_specs,
        out_specs=[pl.BlockSpec((batch, T, LRU_WIDTH), lambda s: (0, cidx(s), 0)),
                   const2((batch, LRU_WIDTH))],
        out_shape=[jax.ShapeDtypeStruct((batch, L, LRU_WIDTH), out_dtype),
                   jax.ShapeDtypeStruct((batch, LRU_WIDTH), F32)],
        scratch_shapes=[pltpu.VMEM((LRU_BLOCKS, batch * S, LANES), F32),
                        pltpu.VMEM((LRU_BLOCKS, batch * S, LANES), F32),
                        pltpu.VMEM((batch, LRU_WIDTH), F32)],
        compiler_params=_params(("arbitrary",)),
        name="lru_rev" if rev else "lru_fwd",
    )(*args)


def _out_proj_kernel(a_ref, m_ref, r_ref, x_ref, mod_ref, g_ref, w_ref, o_ref):
    mix = jnp.concatenate([a_ref[...], m_ref[...], r_ref[...]], axis=-1)
    y = jnp.dot(mix, w_ref[...], preferred_element_type=F32)
    o_ref[...] = _gated_norm_residual(x_ref[...], y, g_ref[...], mod_ref[0][2:3])


def _out_proj(attn, mlp, lru, x, mod, g, w, layer, rows_per_mod, tm):
    M = x.shape[0]
    if rows_per_mod is None:
        mod_map = lambda i: (8, 0, 0)
    else:
        blocks = rows_per_mod // tm
        mod_map = lambda i: (i // blocks, 0, 0)
    return pl.pallas_call(
        _out_proj_kernel,
        grid=(M // tm,),
        in_specs=[
            pl.BlockSpec((tm, ATTN_WIDTH), lambda i: (i, 0)),
            pl.BlockSpec((tm, MLP_WIDTH), lambda i: (i, 0)),
            pl.BlockSpec((tm, LRU_WIDTH), lambda i: (i, 0)),
            pl.BlockSpec((tm, D_MODEL), lambda i: (i, 0)),
            pl.BlockSpec((1, 6, D_MODEL), mod_map),
            pl.BlockSpec((1, D_MODEL), lambda i: (0, 0)),
            pl.BlockSpec((None, D_MODEL, D_MODEL), lambda i: (layer, 0, 0), pipeline_mode=pl.Buffered(1)),
        ],
        out_specs=pl.BlockSpec((tm, D_MODEL), lambda i: (i, 0)),
        out_shape=jax.ShapeDtypeStruct((M, D_MODEL), F32),
        compiler_params=_params(("arbitrary",)),
        name="out_proj",
    )(attn, mlp, lru, x, mod, g, w)


def _ffn_kernel(x_ref, mod_ref, g2_ref, g3_ref, wg_ref, wu_ref, wo_ref, o_ref, h_scr, acc_scr):
    j = pl.program_id(1)
    m = mod_ref[0]

    @pl.when(j == 0)
    def _():
        h = _norm_modulate(x_ref[...], g2_ref[...], m[3:4], m[4:5])
        h_scr[...] = h.astype(BF16)
        acc_scr[...] = jnp.zeros_like(acc_scr)

    h = h_scr[...]
    gate = jnp.dot(h, wg_ref[...], preferred_element_type=F32)
    up = jnp.dot(h, wu_ref[...], preferred_element_type=F32)
    half_gate = 0.5 * gate
    act = (half_gate * (1.0 + jnp.tanh(half_gate)) * up).astype(BF16)
    acc_scr[...] += jnp.dot(act, wo_ref[...], preferred_element_type=F32)

    @pl.when(j == pl.num_programs(1) - 1)
    def _():
        o_ref[...] = _gated_norm_residual(x_ref[...], acc_scr[...], g3_ref[...], m[5:6])


def _ffn(x, mod, g2, g3, w_in, w_out, layer, rows_per_mod, tm, tf):
    M = x.shape[0]
    nf = D_FF // tf
    if rows_per_mod is None:
        mod_map = lambda i, j: (8, 0, 0)
    else:
        blocks = rows_per_mod // tm
        mod_map = lambda i, j: (i // blocks, 0, 0)
    return pl.pallas_call(
        _ffn_kernel,
        grid=(M // tm, nf),
        in_specs=[
            pl.BlockSpec((tm, D_MODEL), lambda i, j: (i, 0)),
            pl.BlockSpec((1, 6, D_MODEL), mod_map),
            pl.BlockSpec((1, D_MODEL), lambda i, j: (0, 0)),
            pl.BlockSpec((1, D_MODEL), lambda i, j: (0, 0)),
            pl.BlockSpec((None, D_MODEL, tf), lambda i, j: (layer, 0, j)),
            pl.BlockSpec((None, D_MODEL, tf), lambda i, j: (layer, 0, j + nf)),
            pl.BlockSpec((None, tf, D_MODEL), lambda i, j: (layer, j, 0)),
        ],
        out_specs=pl.BlockSpec((tm, D_MODEL), lambda i, j: (i, 0)),
        out_shape=jax.ShapeDtypeStruct((M, D_MODEL), F32),
        scratch_shapes=[pltpu.VMEM((tm, D_MODEL), BF16), pltpu.VMEM((tm, D_MODEL), F32)],
        compiler_params=_params(("arbitrary", "arbitrary")),
        name="ffn",
    )(x, mod, g2, g3, w_in, w_in, w_out)


def _rope_tables(n_tokens):
    rows = n_tokens // GRID_W
    row_ids = jnp.repeat(jnp.arange(rows, dtype=F32), GRID_W)
    col_ids = jnp.tile(jnp.arange(GRID_W, dtype=F32), rows)
    inv_freq = ROPE_THETA ** (-jnp.arange(0, AXIS_DIM, 2, dtype=F32) / AXIS_DIM)
    ang_r = row_ids[:, None] * inv_freq
    ang_c = col_ids[:, None] * inv_freq
    ang = jnp.concatenate([ang_r, ang_r, ang_c, ang_c], axis=-1)
    sign = jnp.where((jnp.arange(HEAD_DIM) % AXIS_DIM) < (AXIS_DIM // 2), -1.0, 1.0).astype(F32)
    return jnp.cos(ang), jnp.sin(ang) * sign


def kernel(x, c, ctx, c_ctx, w_mod, b_mod, g_norm, w_in, g_qk, w_s, b_s, conv_w, conv_b,
           lru_w, lru_b, lru_lam, w_out, w_ffn_in, w_ffn_out):
    batch, seq, d = x.shape
    ctx_len = ctx.shape[1]
    depth = w_mod.shape[0]
    cos, sin_signed = _rope_tables(seq)

    c_all = jnp.concatenate([c, c_ctx[None, :], jnp.zeros((MOD_ROWS - batch - 1, d), F32)], axis=0)
    mod_all = _modulation(c_all, w_mod, b_mod).reshape(depth, MOD_ROWS, 6, d)

    xl = x.reshape(batch * seq, d)
    xc = ctx.reshape(batch * ctx_len, d)
    zeros_h = jnp.zeros((batch, LRU_WIDTH), F32)
    w_in_b = w_in.astype(BF16)
    w_out_b = w_out.astype(BF16)
    w_fi_b = w_ffn_in.astype(BF16)
    w_fo_b = w_ffn_out.astype(BF16)

    for l in range(depth):
        last = l == depth - 1
        mod = mod_all[l]
        g = g_norm[l].reshape(4, 1, d)
        ws = w_s[l].astype(BF16)
        bias = jnp.repeat(b_s[l].T, CHUNK, axis=1)
        lw = (0.5 * lru_w[l]).astype(BF16)
        cb = conv_b[l].reshape(1, LRU_WIDTH)

        zl = _in_proj(xl, mod, g[0], w_in_b, l, seq, ROW_TILE)
        zc = _in_proj(xc, mod, g[0], w_in_b, l, None, ROW_TILE)

        attn = _attention(zl, zc, g_qk[l], cos, sin_signed, batch, seq, ctx_len, True, ATTN_TQ)
        mlp = _chunk_mlp(zl, ws, bias, ROW_TILE)

        zc3 = zc.reshape(batch, ctx_len, IN_COLS)
        zl3 = zl.reshape(batch, seq, IN_COLS)
        lru_args = lambda dd: (conv_w[l], cb, lw[dd], lru_b[l, dd], lru_lam[l, dd].reshape(1, LRU_WIDTH))
        hcf, h0f = _lru_pass(zc3, zeros_h, *lru_args(0), rev=False)
        lru_c, h0r = _lru_pass(zc3, zeros_h, *lru_args(1), rev=True, hf=hcf)
        hlf, _ = _lru_pass(zl3, h0f, *lru_args(0), rev=False)
        lru_l, _ = _lru_pass(zl3, h0r, *lru_args(1), rev=True, hf=hlf)

        xl = _out_proj(attn, mlp, lru_l.reshape(batch * seq, LRU_WIDTH), xl, mod, g[1], w_out_b, l, seq,
                       ROW_TILE)
        xl = _ffn(xl, mod, g[2], g[3], w_fi_b, w_fo_b, l, seq, ROW_TILE, FFN_TF)

        if not last:
            attn_c = _attention(zc, zc, g_qk[l], cos, sin_signed, batch, ctx_len, ctx_len, False, ctx_len)
            mlp_c = _chunk_mlp(zc, ws, bias, ROW_TILE)
            xc = _out_proj(attn_c, mlp_c, lru_c.reshape(batch * ctx_len, LRU_WIDTH), xc, mod, g[1], w_out_b,
                           l, None, ROW_TILE)
            xc = _ffn(xc, mod, g[2], g[3], w_fi_b, w_fo_b, l, None, ROW_TILE, FFN_TF)

    return xl.reshape(batch, seq, d)
```
